```python
import math, functools
import jax, jax.numpy as jnp
from jax import lax
import numpy as np

D_MODEL = 1024
BATCH = 8
SEQ = 2048
DEPTH = 2
DEC_BATCH = 128
DEC_SEQ = 8
PAST_LEN = 16384
PAGE_SIZE = 128

F32 = jnp.float32
N_BRANCH = 4
BRANCH_W = D_MODEL // 4
DIFF_HEADS = 4
DIFF_KV_HEADS = 2
DIFF_DH = 32
DIFF_DV = 2 * DIFF_DH
MLA_HEADS = 4
MLA_Q_LORA = 192
MLA_KV_LORA = 128
MLA_NOPE = 32
MLA_ROPE = 16
MLA_DQK = MLA_NOPE + MLA_ROPE
MLA_DV = 64
ROPE_BASE = 10000.0
SSM_GROUP = 16
SSM_GROUPS = BRANCH_W // SSM_GROUP
SSM_STATE = 64
HG_HEADS = 4
HG_DK = BRANCH_W // HG_HEADS
HG_DV = BRANCH_W // HG_HEADS
HG_CHUNK = 16

QBLOCK = 128
EPS = 1e-6

IN_SPLITS = (DIFF_HEADS * 2 * DIFF_DH, DIFF_KV_HEADS * 2 * DIFF_DH, DIFF_KV_HEADS * DIFF_DV,
             MLA_Q_LORA, MLA_KV_LORA, MLA_ROPE,
             BRANCH_W,
             HG_HEADS * HG_DK, HG_HEADS * HG_DK, HG_HEADS * HG_DV,
             N_BRANCH * BRANCH_W,
             N_BRANCH * D_MODEL)
IN_COLS = sum(IN_SPLITS)

kernel_name = 'hybrid_gated_branch_decoder_step'


def rmsnorm(x, g):
    xf = x.astype(F32)
    y = xf * lax.rsqrt(jnp.mean(xf * xf, axis=-1, keepdims=True) + EPS)
    return (y * g.astype(F32)).astype(x.dtype)


def rope(x, pos):
    half = x.shape[-1] // 2
    freqs = ROPE_BASE ** (-jnp.arange(half, dtype=F32) / half)
    ang = pos.astype(F32)[:, None] * freqs
    ang = ang.reshape((ang.shape[0],) + (1,) * (x.ndim - 3) + (half,))
    cos, sin = jnp.cos(ang), jnp.sin(ang)
    x1, x2 = x[..., :half].astype(F32), x[..., half:].astype(F32)
    return jnp.concatenate([x1 * cos - x2 * sin, x2 * cos + x1 * sin], axis=-1).astype(x.dtype)


def over_query_blocks(fn, qs, q_pos):
    lq = q_pos.shape[0]
    if lq <= QBLOCK or lq % QBLOCK:
        return fn(qs, q_pos)
    nb = lq // QBLOCK

    def one(i):
        st = i * QBLOCK
        qb = tuple(lax.dynamic_slice_in_dim(q, st, QBLOCK, axis=q.ndim - 2) for q in qs)
        return fn(qb, lax.dynamic_slice_in_dim(q_pos, st, QBLOCK))

    out = lax.map(one, jnp.arange(nb))
    out = jnp.moveaxis(out, 0, -3)
    return out.reshape(out.shape[:-3] + (lq, out.shape[-1]))


def alibi_slopes(n):
    return 2.0 ** (-8.0 * jnp.arange(1, n + 1, dtype=F32) / n)


def diff_attn_core(qs, q_pos, k1, k2, v, k_pos, lam):
    q1, q2 = qs
    slopes = alibi_slopes(DIFF_HEADS).reshape(DIFF_KV_HEADS, -1)
    dist = (q_pos[:, None] - k_pos[None, :]).astype(F32)
    bias = jnp.where(dist >= 0, -slopes[:, :, None, None] * dist, -jnp.inf)
    scale = DIFF_DH ** -0.5
    s1 = jnp.einsum('bkgqd,bksd->bkgqs', q1, k1).astype(F32) * scale + bias
    s2 = jnp.einsum('bkgqd,bksd->bkgqs', q2, k2).astype(F32) * scale + bias
    a = jax.nn.softmax(s1, axis=-1) - lam * jax.nn.softmax(s2, axis=-1)
    return jnp.einsum('bkgqs,bksd->bkgqd', a, v.astype(F32)).astype(v.dtype)


def diff_branch(xq, xk, xv, pos, p, past, layer_idx):
    b, l, _ = xq.shape
    g = DIFF_HEADS // DIFF_KV_HEADS
    q = rmsnorm(xq.reshape(b, l, DIFF_KV_HEADS, g, 2, DIFF_DH), p['diff_q_gain'])
    k = rmsnorm(xk.reshape(b, l, DIFF_KV_HEADS, 2, DIFF_DH), p['diff_k_gain'])
    new_k = k.reshape(b, l, DIFF_KV_HEADS, 2 * DIFF_DH)
    new_v = xv.reshape(b, l, DIFF_KV_HEADS, DIFF_DV)
    if past is None:
        k_all, v_all = new_k, new_v
    else:
        k_all = jnp.concatenate([past['diff_k'].astype(new_k.dtype), new_k], axis=1)
        v_all = jnp.concatenate([past['diff_v'].astype(new_v.dtype), new_v], axis=1)
    k_pos = jnp.arange(k_all.shape[1])
    kk = k_all.reshape(b, -1, DIFF_KV_HEADS, 2, DIFF_DH)
    k1 = kk[..., 0, :].transpose(0, 2, 1, 3)
    k2 = kk[..., 1, :].transpose(0, 2, 1, 3)
    vt = v_all.transpose(0, 2, 1, 3)
    q1 = q[..., 0, :].transpose(0, 2, 3, 1, 4)
    q2 = q[..., 1, :].transpose(0, 2, 3, 1, 4)
    lam_init = 0.8 - 0.6 * math.exp(-0.3 * layer_idx)
    lp = p['diff_lambda'].astype(F32)
    lam = jnp.exp(jnp.sum(lp[0] * lp[1])) - jnp.exp(jnp.sum(lp[2] * lp[3])) + lam_init
    o = over_query_blocks(lambda qs, qp: diff_attn_core(qs, qp, k1, k2, vt, k_pos, lam), (q1, q2), pos)
    o = rmsnorm(o, p['diff_subln_gain']) * (1.0 - lam_init)
    o = o.transpose(0, 3, 1, 2, 4).reshape(b, l, DIFF_HEADS * DIFF_DV)
    return o.astype(xq.dtype), new_k, new_v


def mla_keys(c, kr, w_uk, g):
    k_nope = jnp.einsum('bsc,chd->bhsd', c, w_uk)
    k_rope = jnp.broadcast_to(kr[:, None], (kr.shape[0], MLA_HEADS) + kr.shape[1:]).astype(k_nope.dtype)
    return rmsnorm(jnp.concatenate([k_nope, k_rope], axis=-1), g)


def mla_core(qs, q_pos, k, c, w_uv, k_pos):
    (q,) = qs
    mask = k_pos[None, :] <= q_pos[:, None]
    s = jnp.einsum('bhqd,bhsd->bhqs', q, k).astype(F32) * (MLA_DQK ** -0.5)
    pr = jax.nn.softmax(jnp.where(mask, s, -jnp.inf), axis=-1)
    o_lat = jnp.einsum('bhqs,bsc->bhqc', pr, c.astype(F32))
    return jnp.einsum('bhqc,chd->bhqd', o_lat, w_uv.astype(F32)).astype(c.dtype)


def mla_branch(xqa, xkva, xkr, pos, p, past):
    b, l, _ = xqa.shape
    q = (rmsnorm(xqa, p['mla_qa_gain']) @ p['w_mla_uq']).reshape(b, l, MLA_HEADS, MLA_DQK)
    q = jnp.concatenate([q[..., :MLA_NOPE], rope(q[..., MLA_NOPE:], pos)], axis=-1)
    q = rmsnorm(q, p['mla_q_gain']).transpose(0, 2, 1, 3)
    c = rmsnorm(xkva, p['mla_kva_gain'])
    kr = rope(xkr, pos)
    if past is None:
        c_all, kr_all = c, kr
    else:
        c_all = jnp.concatenate([past['mla_c'].astype(c.dtype), c], axis=1)
        kr_all = jnp.concatenate([past['mla_kr'].astype(kr.dtype), kr], axis=1)
    k_pos = jnp.arange(c_all.shape[1])
    k = mla_keys(c_all, kr_all, p['w_mla_uk'], p['mla_k_gain'])
    w_uv = p['w_mla_uv']
    o = over_query_blocks(lambda qs, qp: mla_core(qs, qp, k, c_all, w_uv, k_pos), (q,), pos)
    o = o.transpose(0, 2, 1, 3).reshape(b, l, MLA_HEADS * MLA_DV)
    return o.astype(xqa.dtype), c, kr


def s5_branch(u, p, h0):
    b, l, _ = u.shape
    lam = lax.complex(p['ssm_a_re'].astype(F32), p['ssm_a_im'].astype(F32))
    dt = jnp.exp(p['ssm_log_dt'].astype(F32))[:, None]
    a_bar = jnp.exp(lam * dt)
    bmat = lax.complex(p['ssm_b_re'].astype(F32), p['ssm_b_im'].astype(F32))
    b_bar = ((a_bar - 1.0) / lam)[..., None] * bmat
    cmat = lax.complex(p['ssm_c_re'].astype(F32), p['ssm_c_im'].astype(F32))
    ug = u.reshape(b, l, SSM_GROUPS, SSM_GROUP).astype(F32)
    bu = jnp.einsum('blgh,gph->blgp', ug, b_bar)
    if h0 is not None:
        bu = bu.at[:, 0].add(a_bar * h0)
    a_seq = jnp.broadcast_to(a_bar, bu.shape)

    def combine(e1, e2):
        a1, b1 = e1
        a2, b2 = e2
        return a1 * a2, a2 * b1 + b2

    _, h = lax.associative_scan(combine, (a_seq, bu), axis=1)
    y = jnp.einsum('blgp,ghp->blgh', h, cmat).real + p['ssm_d'].astype(F32).reshape(SSM_GROUPS, SSM_GROUP) * ug
    y = y.reshape(b, l, BRANCH_W).astype(u.dtype)
    ya, yb = jnp.split(y @ p['w_glu'], 2, axis=-1)
    return ya * jax.nn.sigmoid(yb), h[:, -1]


def hgrn2_branch(xq, xf, xi, lb, p, s0):
    b, l, _ = xq.shape
    q = xq.reshape(b, l, HG_HEADS, HG_DK).astype(F32)
    zf = xf.reshape(b, l, HG_HEADS, HG_DK).astype(F32)
    lbh = lb.reshape(HG_HEADS, HG_DK)
    log_f = jnp.log(lbh + (1.0 - lbh) * jax.nn.sigmoid(zf))
    k = (1.0 - lbh) * jax.nn.sigmoid(-zf)
    v = xi.reshape(b, l, HG_HEADS, HG_DV).astype(F32)
    c = math.gcd(l, HG_CHUNK)
    n = l // c

    def chunks(t):
        return t.reshape(b, n, c, HG_HEADS, -1).transpose(0, 3, 1, 2, 4)

    q, k, v, log_f = chunks(q), chunks(k), chunks(v), chunks(log_f)
    bcum = jnp.cumsum(log_f, axis=3)
    causal = jnp.tril(jnp.ones((c, c), dtype=bool))
    diff = bcum[:, :, :, :, None, :] - bcum[:, :, :, None, :, :]
    decay = jnp.exp(jnp.where(causal[:, :, None], diff, -jnp.inf))
    a = jnp.einsum('bhntd,bhnsd,bhntsd->bhnts', q, k, decay)
    o_intra = jnp.einsum('bhnts,bhnsv->bhntv', a, v)
    b_last = bcum[:, :, :, -1]
    k_dec = k * jnp.exp(b_last[:, :, :, None] - bcum)
    chunk_kv = jnp.einsum('bhnsd,bhnsv->bhndv', k_dec, v)

    def step(s, inp):
        bl, kv = inp
        return jnp.exp(bl)[..., None] * s + kv, s

    s_fin, s_prev = lax.scan(step, s0, (jnp.moveaxis(b_last, 2, 0), jnp.moveaxis(chunk_kv, 2, 0)))
    s_prev = jnp.moveaxis(s_prev, 0, 2)
    o_inter = jnp.einsum('bhntd,bhndv->bhntv', q * jnp.exp(bcum), s_prev)
    o = (o_intra + o_inter).transpose(0, 2, 3, 1, 4).reshape(b, l, HG_HEADS, HG_DV)
    o = rmsnorm(o, p['hg_norm_gain']).reshape(b, l, HG_HEADS * HG_DV)
    return o.astype(xq.dtype), s_fin


def mixer_layer(x, pos, p, layer_idx, lb, past):
    b, l, _ = x.shape
    h = rmsnorm(x, p['norm_gain']) @ p['w_in']
    (dq, dk, dv, mqa, mkva, mkr, su, hq, hf, hi, gates, merge) = jnp.split(
        h, np.cumsum(IN_SPLITS)[:-1].tolist(), axis=-1)
    o_diff, k_new, v_new = diff_branch(dq, dk, dv, pos, p, past, layer_idx)
    o_mla, c_new, kr_new = mla_branch(mqa, mkva, mkr, pos, p, past)
    o_ssm, h_fin = s5_branch(su, p, None if past is None else past['ssm'])
    s0 = jnp.zeros((b, HG_HEADS, HG_DK, HG_DV), F32) if past is None else past['hgrn']
    o_hg, s_fin = hgrn2_branch(hq, hf, hi, lb, p, s0)
    br = jnp.stack([o_diff, o_mla, o_ssm, o_hg], axis=2)
    br = br * jax.nn.silu(gates.reshape(b, l, N_BRANCH, BRANCH_W))
    up = jnp.einsum('blkw,kwd->blkd', br, p['w_branch'])
    m = jnp.sum(jax.nn.sigmoid(merge.reshape(b, l, N_BRANCH, D_MODEL)) * up, axis=2)
    return x + m @ p['w_out'], (k_new, v_new, c_new, kr_new, h_fin, s_fin)


def setup_inputs(seed: int = 0) -> dict:
    key = jax.random.key(seed)
    keys = jax.random.split(key, 48)
    counter = [0]

    def nk():
        counter[0] += 1
        return keys[counter[0] - 1]

    def nrm(shape, scale=1.0):
        return jax.random.normal(nk(), shape, F32) * scale

    n_pages = PAST_LEN // PAGE_SIZE
    n_pool = (DEC_BATCH * n_pages * 5) // 4
    page_table = jax.random.permutation(nk(), n_pool)[:DEC_BATCH * n_pages]
    page_table = page_table.reshape(DEC_BATCH, n_pages).astype(jnp.int32)
    dd = DEPTH
    return {
        'x_prompt': nrm((BATCH, SEQ, D_MODEL)),
        'x_sample': nrm((DEC_BATCH, DEC_SEQ, D_MODEL)),
        'cache_diff_k': nrm((dd, n_pool, PAGE_SIZE, DIFF_KV_HEADS, 2 * DIFF_DH)),
        'cache_diff_v': nrm((dd, n_pool, PAGE_SIZE, DIFF_KV_HEADS, DIFF_DV)),
        'cache_mla_c': nrm((dd, n_pool, PAGE_SIZE, MLA_KV_LORA)),
        'cache_mla_kr': nrm((dd, n_pool, PAGE_SIZE, MLA_ROPE)),
        'state_ssm_re': nrm((dd, DEC_BATCH, SSM_GROUPS, SSM_STATE), 0.3),
        'state_ssm_im': nrm((dd, DEC_BATCH, SSM_GROUPS, SSM_STATE), 0.3),
        'state_hgrn': nrm((dd, DEC_BATCH, HG_HEADS, HG_DK, HG_DV), 0.3),
        'page_table': page_table,
        'norm_gain': 1.0 + nrm((dd, D_MODEL), 0.02),
        'w_in': nrm((dd, D_MODEL, IN_COLS), D_MODEL ** -0.5),
        'w_branch': nrm((dd, N_BRANCH, BRANCH_W, D_MODEL), BRANCH_W ** -0.5),
        'w_out': nrm((dd, D_MODEL, D_MODEL), 0.5 * D_MODEL ** -0.5),
        'diff_q_gain': 1.0 + nrm((dd, DIFF_DH), 0.02),
        'diff_k_gain': 1.0 + nrm((dd, DIFF_DH), 0.02),
        'diff_lambda': nrm((dd, 4, DIFF_DH), 0.1),
        'diff_subln_gain': 1.0 + nrm((dd, DIFF_DV), 0.02),
        'mla_qa_gain': 1.0 + nrm((dd, MLA_Q_LORA), 0.02),
        'mla_kva_gain': 1.0 + nrm((dd, MLA_KV_LORA), 0.02),
        'w_mla_uq': nrm((dd, MLA_Q_LORA, MLA_HEADS * MLA_DQK), MLA_Q_LORA ** -0.5),
        'w_mla_uk': nrm((dd, MLA_KV_LORA, MLA_HEADS, MLA_NOPE), MLA_KV_LORA ** -0.5),
        'w_mla_uv': nrm((dd, MLA_KV_LORA, MLA_HEADS, MLA_DV), MLA_KV_LORA ** -0.5),
        'mla_q_gain': 1.0 + nrm((dd, MLA_DQK), 0.02),
        'mla_k_gain': 1.0 + nrm((dd, MLA_DQK), 0.02),
        'ssm_a_re': -0.5 + nrm((dd, SSM_GROUPS, SSM_STATE), 0.01),
        'ssm_a_im': jnp.pi * jnp.arange(SSM_STATE, dtype=F32) + nrm((dd, SSM_GROUPS, SSM_STATE), 0.01),
        'ssm_log_dt': jax.random.uniform(nk(), (dd, SSM_GROUPS), F32, math.log(1e-3), math.log(1e-1)),
        'ssm_b_re': nrm((dd, SSM_GROUPS, SSM_STATE, SSM_GROUP), (2 * SSM_GROUP) ** -0.5),
        'ssm_b_im': nrm((dd, SSM_GROUPS, SSM_STATE, SSM_GROUP), (2 * SSM_GROUP) ** -0.5),
        'ssm_c_re': nrm((dd, SSM_GROUPS, SSM_GROUP, SSM_STATE), SSM_STATE ** -0.5),
        'ssm_c_im': nrm((dd, SSM_GROUPS, SSM_GROUP, SSM_STATE), SSM_STATE ** -0.5),
        'ssm_d': nrm((dd, BRANCH_W)),
        'w_glu': nrm((dd, BRANCH_W, 2 * BRANCH_W), BRANCH_W ** -0.5),
        'hg_lb_logits': nrm((dd, BRANCH_W), 0.1),
        'hg_norm_gain': 1.0 + nrm((dd, HG_DV), 0.02),
    }


def reference(x_prompt, x_sample, cache_diff_k, cache_diff_v, cache_mla_c, cache_mla_kr,
              state_ssm_re, state_ssm_im, state_hgrn, page_table,
              norm_gain, w_in, w_branch, w_out,
              diff_q_gain, diff_k_gain, diff_lambda, diff_subln_gain,
              mla_qa_gain, mla_kva_gain, w_mla_uq, w_mla_uk, w_mla_uv, mla_q_gain, mla_k_gain,
              ssm_a_re, ssm_a_im, ssm_log_dt, ssm_b_re, ssm_b_im, ssm_c_re, ssm_c_im, ssm_d, w_glu,
              hg_lb_logits, hg_norm_gain):
    sm = jax.nn.softmax(hg_lb_logits.astype(F32), axis=0)
    lb_all = jnp.cumsum(sm, axis=0) - sm[0]
    pos_p = jnp.arange(x_prompt.shape[1])
    pos_s = PAST_LEN + jnp.arange(x_sample.shape[1])
    n_seq = page_table.shape[0]

    def gather(cache):
        g = cache[page_table]
        return g.reshape((n_seq, -1) + cache.shape[2:])

    hp, hs = x_prompt, x_sample
    outs_p, outs_s = [], []
    for l in range(DEPTH):
        p = dict(norm_gain=norm_gain[l], w_in=w_in[l], w_branch=w_branch[l], w_out=w_out[l],
                 diff_q_gain=diff_q_gain[l], diff_k_gain=diff_k_gain[l], diff_lambda=diff_lambda[l],
                 diff_subln_gain=diff_subln_gain[l],
                 mla_qa_gain=mla_qa_gain[l], mla_kva_gain=mla_kva_gain[l], w_mla_uq=w_mla_uq[l],
                 w_mla_uk=w_mla_uk[l], w_mla_uv=w_mla_uv[l], mla_q_gain=mla_q_gain[l], mla_k_gain=mla_k_gain[l],
                 ssm_a_re=ssm_a_re[l], ssm_a_im=ssm_a_im[l], ssm_log_dt=ssm_log_dt[l],
                 ssm_b_re=ssm_b_re[l], ssm_b_im=ssm_b_im[l], ssm_c_re=ssm_c_re[l], ssm_c_im=ssm_c_im[l],
                 ssm_d=ssm_d[l], w_glu=w_glu[l], hg_norm_gain=hg_norm_gain[l])
        hp, st_p = mixer_layer(hp, pos_p, p, l, lb_all[l], None)
        past = dict(diff_k=gather(cache_diff_k[l]), diff_v=gather(cache_diff_v[l]),
                    mla_c=gather(cache_mla_c[l]), mla_kr=gather(cache_mla_kr[l]),
                    ssm=lax.complex(state_ssm_re[l].astype(F32), state_ssm_im[l].astype(F32)),
                    hgrn=state_hgrn[l].astype(F32))
        hs, st_s = mixer_layer(hs, pos_s, p, l, lb_all[l], past)
        outs_p.append(st_p)
        outs_s.append(st_s)

    def stack(group, i):
        return jnp.stack([st[i] for st in group], axis=0)

    def to_pages(a):
        return a.reshape(a.shape[:2] + (a.shape[2] // PAGE_SIZE, PAGE_SIZE) + a.shape[3:])

    diff_k_p = to_pages(stack(outs_p, 0))
    diff_v_p = to_pages(stack(outs_p, 1))
    mla_c_p = to_pages(stack(outs_p, 2))
    mla_kr_p = to_pages(stack(outs_p, 3))
    ssm_p = stack(outs_p, 4)
    hgrn_p = stack(outs_p, 5)
    diff_k_s = stack(outs_s, 0)
    diff_v_s = stack(outs_s, 1)
    mla_c_s = stack(outs_s, 2)
    mla_kr_s = stack(outs_s, 3)
    ssm_s = stack(outs_s, 4)
    hgrn_s = stack(outs_s, 5)
    return (hp, hs,
            diff_k_p, diff_v_p, mla_c_p, mla_kr_p, jnp.real(ssm_p), jnp.imag(ssm_p), hgrn_p,
            diff_k_s, diff_v_s, mla_c_s, mla_kr_s, jnp.real(ssm_s), jnp.imag(ssm_s), hgrn_s)
```

```python
import functools
import math

import numpy as np
import jax
import jax.numpy as jnp
from jax import lax
from jax.experimental import pallas as pl
from jax.experimental.pallas import tpu as pltpu

F32 = jnp.float32
BF16 = jnp.bfloat16
EPS = 1e-6
NEG = -1e30

D_MODEL = 1024
BRANCH_W = 256
N_BRANCH = 4
DIFF_HEADS, DIFF_KVH, DIFF_DH, DIFF_DV = 4, 2, 32, 64
MLA_HEADS, MLA_QL, MLA_KVL, MLA_NOPE, MLA_ROPE, MLA_DV = 4, 192, 128, 32, 16, 64
MLA_DQK = MLA_NOPE + MLA_ROPE
ROPE_BASE = 10000.0
SSM_GROUP, SSM_GROUPS, SSM_STATE = 16, 16, 64
SSM_W = SSM_GROUPS * SSM_STATE
HG_HEADS, HG_DK, HG_DV, HG_CHUNK = 4, 64, 64, 16
PAGE = 128
LANE = 128
SUB = 8
VMEM_LIMIT = 56 * 1024 * 1024

_OFF = dict(dq=0, dk=256, dv=384, mqa=512, mkva=704, mkr=832, su=848, hq=1104, hf=1360, hi=1616,
            gates=1872, merge=2896)
IN_COLS = 6992
_A = dict(dq=0, dk=256, dv=384, mqa=512, mkva=768, krt=896, kr0=1152, su=1280, hq=1536, hf=1792, hi=2048)
NA = 2304


def _dot(a, b):
    return jnp.dot(a, b, preferred_element_type=F32)


def _dot_nt(a, b):
    return lax.dot_general(a, b, (((1,), (1,)), ((), ())), preferred_element_type=F32)


def _dot_tn(a, b):
    return lax.dot_general(a, b, (((0,), (0,)), ((), ())), preferred_element_type=F32)


def _split2(x):
    hi = x.astype(BF16)
    lo = (x - hi.astype(F32)).astype(BF16)
    return hi, lo


def _split3(x):
    hi = x.astype(BF16)
    r = x - hi.astype(F32)
    mid = r.astype(BF16)
    lo = (r - mid.astype(F32)).astype(BF16)
    return hi, mid, lo


def _dot2(x, w):
    hi, lo = _split2(x)
    return _dot(hi, w) + _dot(lo, w)


def _dot3(x, w):
    hi, mid, lo = _split3(x)
    return _dot(hi, w) + _dot(mid, w) + _dot(lo, w)


def _sigmoid(x):
    return 1.0 / (1.0 + jnp.exp(-x))


def _rope_lanes(x, cs, sn, x1mask):
    rot = jnp.where(x1mask, pltpu.roll(x, LANE - MLA_ROPE // 2, 1), pltpu.roll(x, MLA_ROPE // 2, 1))
    return x * cs + rot * sn


def _inproj_kernel(sample, *refs):
    (x_ref, g_ref, wa_ref, gq_ref, gk_ref, gqa_ref, wuq_ref, gq48_ref, gkva_ref, wukp_ref, gk48_ref,
     cs_ref, sn_ref, cs0_ref, sn0_ref, ones32_ref, ones64_ref) = refs[:17]
    if sample:
        wabs_ref, selr_ref = refs[17:19]
        (qd_ref, kd_ref, vd_ref, c_ref, kr16_ref, kr128_ref, qabs_ref, qr_ref,
         su_ref, hq_ref, hf_ref, hi_ref) = refs[19:]
    else:
        (qd_ref, kd_ref, vd_ref, qm_ref, c_ref, kmla_ref, kr16_ref,
         su_ref, hq_ref, hf_ref, hi_ref) = refs[17:]

    x = x_ref[...]
    ms = jnp.mean(x * x, axis=-1, keepdims=True)
    xn = (x * lax.rsqrt(ms + EPS) * g_ref[...]).astype(BF16)
    h = _dot(xn, wa_ref[...])

    def seg(name, width):
        return h[:, _A[name]:_A[name] + width]

    dq = seg('dq', 256)
    msq = _dot2(dq * dq, ones32_ref[...]) * (1.0 / DIFF_DH)
    qd_ref[...] = dq * lax.rsqrt(msq + EPS) * gq_ref[...]
    dk = seg('dk', 128)
    msk = _dot2(dk * dk, ones32_ref[0:128, 0:128]) * (1.0 / DIFF_DH)
    kd_ref[...] = dk * lax.rsqrt(msk + EPS) * gk_ref[...]
    vd_ref[...] = seg('dv', 128)

    lane = lax.broadcasted_iota(jnp.int32, (1, LANE), 1)
    x1_head = ((lane & 63) >= MLA_NOPE) & ((lane & 63) < MLA_NOPE + MLA_ROPE // 2)
    x1_zero = lane < MLA_ROPE // 2
    cs = cs_ref[...]
    sn = sn_ref[...]

    mqa = seg('mqa', 256)
    msa = jnp.sum(mqa * mqa, axis=-1, keepdims=True) * (1.0 / MLA_QL)
    qa = (mqa * lax.rsqrt(msa + EPS) * gqa_ref[...]).astype(BF16)
    q = _dot(qa, wuq_ref[...])
    q = jnp.concatenate([_rope_lanes(q[:, :LANE], cs, sn, x1_head),
                         _rope_lanes(q[:, LANE:], cs, sn, x1_head)], axis=1)
    msq2 = _dot2(q * q, ones64_ref[...]) * (1.0 / MLA_DQK)
    qm = q * lax.rsqrt(msq2 + EPS) * gq48_ref[...]

    mkva = seg('mkva', 128)
    msc = jnp.mean(mkva * mkva, axis=-1, keepdims=True)
    c = mkva * lax.rsqrt(msc + EPS) * gkva_ref[...]
    c_ref[...] = c

    kr0 = _rope_lanes(seg('kr0', 128), cs0_ref[...], sn0_ref[...], x1_zero)
    kr16_ref[...] = kr0[:, :MLA_ROPE]

    if sample:
        kr128_ref[...] = kr0
        qg = qm * gk48_ref[...]
        qabs_ref[...] = _dot2(qg, wabs_ref[...])
        qr_ref[...] = _dot3(qg, selr_ref[...])
    else:
        qm_ref[...] = qm
        krt = seg('krt', 256)
        krt = jnp.concatenate([_rope_lanes(krt[:, :LANE], cs, sn, x1_head),
                               _rope_lanes(krt[:, LANE:], cs, sn, x1_head)], axis=1)
        kpre = _dot(c.astype(BF16), wukp_ref[...]) + krt
        msk2 = _dot2(kpre * kpre, ones64_ref[...]) * (1.0 / MLA_DQK)
        kmla_ref[...] = kpre * lax.rsqrt(msk2 + EPS) * gk48_ref[...]

    su_ref[...] = seg('su', 256)
    hq_ref[...] = seg('hq', 256)
    hf_ref[...] = seg('hf', 256)
    hi_ref[...] = seg('hi', 256)


def _inproj(x, lw, tabs, tm, n_tab, sample):
    t = x.shape[0]
    assert t % tm == 0
    const = lambda shape: pl.BlockSpec(shape, lambda i: (0,) * len(shape))
    row = lambda w: pl.BlockSpec((tm, w), lambda i: (i, 0))
    tab = pl.BlockSpec((tm, LANE), lambda i: (i % n_tab, 0))
    ins = [x, lw['g'], lw['wa'], lw['gq'], lw['gk'], lw['gqa'], lw['wuq'], lw['gq48'], lw['gkva'], lw['wukp'],
           lw['gk48'], tabs['cs'], tabs['sn'], tabs['cs0'], tabs['sn0'], lw['ones32'], lw['ones64']]
    in_specs = [row(D_MODEL), const((1, D_MODEL)), const((D_MODEL, NA)), const((1, 256)), const((1, 128)),
                const((1, 256)), const((256, 256)), const((1, 256)), const((1, 128)), const((128, 256)),
                const((1, 256)), tab, tab, tab, tab, const((256, 256)), const((256, 256))]
    if sample:
        ins += [lw['wabs'], lw['selr']]
        in_specs += [const((256, 512)), const((256, 512))]
        widths = [256, 128, 128, 128, MLA_ROPE, 128, 512, 512, 256, 256, 256, 256]
    else:
        widths = [256, 128, 128, 256, 128, 256, MLA_ROPE, 256, 256, 256, 256]
    return pl.pallas_call(
        functools.partial(_inproj_kernel, sample),
        grid=(t // tm,),
        in_specs=in_specs,
        out_specs=[row(w) for w in widths],
        out_shape=[jax.ShapeDtypeStruct((t, w), F32) for w in widths],
        compiler_params=pltpu.CompilerParams(dimension_semantics=("arbitrary",), vmem_limit_bytes=VMEM_LIMIT),
        name="inproj_sample" if sample else "inproj_prompt",
    )(*ins)


def _merge_kernel(x_ref, g_ref, wb_ref, b0_ref, b1_ref, b2_ref, b3_ref, wbr_ref, wout_ref, y_ref):
    x = x_ref[...]
    ms = jnp.mean(x * x, axis=-1, keepdims=True)
    xn = (x * lax.rsqrt(ms + EPS) * g_ref[...]).astype(BF16)
    m = jnp.zeros(x.shape, F32)
    for k, b_ref in enumerate((b0_ref, b1_ref, b2_ref, b3_ref)):
        gates = _dot(xn, wb_ref[:, k * BRANCH_W:(k + 1) * BRANCH_W])
        br = b_ref[...] * (gates * _sigmoid(gates))
        up = _dot(br.astype(BF16), wbr_ref[k])
        mg = _dot(xn, wb_ref[:, N_BRANCH * BRANCH_W + k * D_MODEL:N_BRANCH * BRANCH_W + (k + 1) * D_MODEL])
        m = m + _sigmoid(mg) * up
    y_ref[...] = x + _dot(m.astype(BF16), wout_ref[...])


def _merge(x, lw, branches, tm):
    t = x.shape[0]
    const = lambda shape: pl.BlockSpec(shape, lambda i: (0,) * len(shape))
    row = lambda w: pl.BlockSpec((tm, w), lambda i: (i, 0))
    nb = N_BRANCH * BRANCH_W + N_BRANCH * D_MODEL
    return pl.pallas_call(
        _merge_kernel,
        grid=(t // tm,),
        in_specs=[row(D_MODEL), const((1, D_MODEL)), const((D_MODEL, nb)), row(256), row(256), row(256), row(256),
                  const((N_BRANCH, BRANCH_W, D_MODEL)), const((D_MODEL, D_MODEL))],
        out_specs=row(D_MODEL),
        out_shape=jax.ShapeDtypeStruct((t, D_MODEL), F32),
        compiler_params=pltpu.CompilerParams(dimension_semantics=("arbitrary",), vmem_limit_bytes=VMEM_LIMIT),
        name="merge",
    )(x, lw['g'], lw['wb'], *branches, lw['wbr'], lw['wout'])


def _diff_slope(g, kvh):
    head = kvh * (DIFF_HEADS // DIFF_KVH) + g
    return 2.0 ** (-8.0 * (head + 1) / DIFF_HEADS)


def _diff_finish(acc_ref, l_ref, lam, gsub, ones64, lam_init, rows):
    lane = lax.broadcasted_iota(jnp.int32, (1, LANE), 1)
    outs = []
    for g in range(2):
        per_kvh = []
        for kvh in range(2):
            r1 = (g * 4 + kvh * 2) * rows
            r2 = r1 + rows
            o1 = acc_ref[r1:r1 + rows, :] / l_ref[r1:r1 + rows, :]
            o2 = acc_ref[r2:r2 + rows, :] / l_ref[r2:r2 + rows, :]
            per_kvh.append(o1 - lam * o2)
        og = jnp.where(lane < DIFF_DV, per_kvh[0], per_kvh[1])
        msq = _dot2(og * og, ones64) * (1.0 / DIFF_DV)
        outs.append(og * lax.rsqrt(msq + EPS) * gsub * (1.0 - lam_init))
    return outs


def _fill_diff_q(qbig_ref, q, rows):
    lane = lax.broadcasted_iota(jnp.int32, (1, LANE), 1)
    for g in range(2):
        qg = q[:, g * LANE:(g + 1) * LANE]
        for kvh in range(2):
            for mp in range(2):
                lo = kvh * 64 + mp * 32
                r = g * 4 + kvh * 2 + mp
                qbig_ref[r * rows:(r + 1) * rows, :] = jnp.where(
                    (lane >= lo) & (lane < lo + DIFF_DH), qg, 0.0).astype(BF16)


def _softmax_update(s, m_ref, l_ref, acc_ref, r0, rows, pv):
    m_old = m_ref[r0:r0 + rows, :]
    m_new = jnp.maximum(m_old, jnp.max(s, axis=-1, keepdims=True))
    alpha = jnp.exp(m_old - m_new)
    p = jnp.exp(s - m_new)
    l_ref[r0:r0 + rows, :] = alpha * l_ref[r0:r0 + rows, :] + jnp.sum(p, axis=-1, keepdims=True)
    acc_ref[r0:r0 + rows, :] = alpha * acc_ref[r0:r0 + rows, :] + pv(p.astype(BF16))
    m_ref[r0:r0 + rows, :] = m_new


def _diffp_kernel(tq, tk, lam_init, lam_ref, q_ref, k_ref, v_ref, gsub_ref, ones64_ref, o_ref,
                  qbig_ref, m_ref, l_ref, acc_ref):
    qi = pl.program_id(1)
    _fill_diff_q(qbig_ref, q_ref[0], tq)
    m_ref[...] = jnp.full(m_ref.shape, NEG, F32)
    l_ref[...] = jnp.zeros(l_ref.shape, F32)
    acc_ref[...] = jnp.zeros(acc_ref.shape, F32)
    scale = DIFF_DH ** -0.5
    qpos = qi * tq + lax.broadcasted_iota(jnp.int32, (tq, 1), 0)

    def kv_step(j, carry):
        kt = k_ref[0, pl.ds(j * tk, tk), :].astype(BF16)
        vt = v_ref[0, pl.ds(j * tk, tk), :].astype(BF16)
        s_all = _dot_nt(qbig_ref[...], kt)
        kpos = j * tk + lax.broadcasted_iota(jnp.int32, (1, tk), 1)
        rel = (kpos - qpos).astype(F32)
        valid = kpos <= qpos
        for g in range(2):
            for kvh in range(2):
                for mp in range(2):
                    r = g * 4 + kvh * 2 + mp
                    s = s_all[r * tq:(r + 1) * tq, :] * scale + _diff_slope(g, kvh) * rel
                    s = jnp.where(valid, s, NEG)
                    _softmax_update(s, m_ref, l_ref, acc_ref, r * tq, tq, lambda p: _dot(p, vt))
        return carry

    n_kv = ((qi + 1) * tq + tk - 1) // tk
    lax.fori_loop(0, n_kv, kv_step, 0)
    outs = _diff_finish(acc_ref, l_ref, lam_ref[0], gsub_ref[...], ones64_ref[...], lam_init, tq)
    o_ref[0, :, 0:LANE] = outs[0]
    o_ref[0, :, LANE:2 * LANE] = outs[1]


def _diff_prompt(qd, kd, vd, lam, lw, lam_init, b, l):
    tq = min(128, l)
    tk = min(256, l)
    kern = functools.partial(_diffp_kernel, tq, tk, lam_init)
    return pl.pallas_call(
        kern,
        grid=(b, l // tq),
        in_specs=[pl.BlockSpec(memory_space=pltpu.SMEM),
                  pl.BlockSpec((1, tq, 256), lambda bi, qi: (bi, qi, 0)),
                  pl.BlockSpec((1, l, 128), lambda bi, qi: (bi, 0, 0)),
                  pl.BlockSpec((1, l, 128), lambda bi, qi: (bi, 0, 0)),
                  pl.BlockSpec((1, 128), lambda bi, qi: (0, 0)),
                  pl.BlockSpec((128, 128), lambda bi, qi: (0, 0))],
        out_specs=pl.BlockSpec((1, tq, 256), lambda bi, qi: (bi, qi, 0)),
        out_shape=jax.ShapeDtypeStruct((b, l, 256), F32),
        scratch_shapes=[pltpu.VMEM((8 * tq, LANE), BF16), pltpu.VMEM((8 * tq, 1), F32),
                        pltpu.VMEM((8 * tq, 1), F32), pltpu.VMEM((8 * tq, LANE), F32)],
        compiler_params=pltpu.CompilerParams(dimension_semantics=("arbitrary", "arbitrary"),
                                             vmem_limit_bytes=VMEM_LIMIT),
        name="diff_prompt",
    )(lam, qd.reshape(b, l, 256), kd.reshape(b, l, 128), vd.reshape(b, l, 128), lw['gsub'], lw['ones64_128'])


def _fill_mla_q(qbig_ref, q, rows):
    lane = lax.broadcasted_iota(jnp.int32, (1, 2 * LANE), 1)
    for h in range(MLA_HEADS):
        qbig_ref[h * rows:(h + 1) * rows, :] = jnp.where(
            (lane >= h * 64) & (lane < (h + 1) * 64), q, 0.0).astype(BF16)


def _mla_finish(acc_ref, l_ref, wuv_ref, rows):
    o = jnp.zeros((rows, 2 * LANE), F32)
    for h in range(MLA_HEADS):
        olat = acc_ref[h * rows:(h + 1) * rows, :] / l_ref[h * rows:(h + 1) * rows, :]
        o = o + _dot(olat.astype(BF16), wuv_ref[h])
    return o


def _mlap_kernel(tq, tk, q_ref, k_ref, c_ref, wuv_ref, o_ref, qbig_ref, m_ref, l_ref, acc_ref):
    qi = pl.program_id(1)
    _fill_mla_q(qbig_ref, q_ref[0], tq)
    m_ref[...] = jnp.full(m_ref.shape, NEG, F32)
    l_ref[...] = jnp.zeros(l_ref.shape, F32)
    acc_ref[...] = jnp.zeros(acc_ref.shape, F32)
    scale = MLA_DQK ** -0.5
    qpos = qi * tq + lax.broadcasted_iota(jnp.int32, (tq, 1), 0)

    def kv_step(j, carry):
        kt = k_ref[0, pl.ds(j * tk, tk), :].astype(BF16)
        ct = c_ref[0, pl.ds(j * tk, tk), :].astype(BF16)
        s_all = _dot_nt(qbig_ref[...], kt)
        kpos = j * tk + lax.broadcasted_iota(jnp.int32, (1, tk), 1)
        valid = kpos <= qpos
        for h in range(MLA_HEADS):
            s = jnp.where(valid, s_all[h * tq:(h + 1) * tq, :] * scale, NEG)
            _softmax_update(s, m_ref, l_ref, acc_ref, h * tq, tq, lambda p: _dot(p, ct))
        return carry

    n_kv = ((qi + 1) * tq + tk - 1) // tk
    lax.fori_loop(0, n_kv, kv_step, 0)
    o_ref[0] = _mla_finish(acc_ref, l_ref, wuv_ref, tq)


def _mla_prompt(qm, kmla, c, lw, b, l):
    tq = min(128, l)
    tk = min(256, l)
    return pl.pallas_call(
        functools.partial(_mlap_kernel, tq, tk),
        grid=(b, l // tq),
        in_specs=[pl.BlockSpec((1, tq, 256), lambda bi, qi: (bi, qi, 0)),
                  pl.BlockSpec((1, l, 256), lambda bi, qi: (bi, 0, 0)),
                  pl.BlockSpec((1, l, 128), lambda bi, qi: (bi, 0, 0)),
                  pl.BlockSpec((MLA_HEADS, 128, 256), lambda bi, qi: (0, 0, 0))],
        out_specs=pl.BlockSpec((1, tq, 256), lambda bi, qi: (bi, qi, 0)),
        out_shape=jax.ShapeDtypeStruct((b, l, 256), F32),
        scratch_shapes=[pltpu.VMEM((MLA_HEADS * tq, 2 * LANE), BF16), pltpu.VMEM((MLA_HEADS * tq, 1), F32),
                        pltpu.VMEM((MLA_HEADS * tq, 1), F32), pltpu.VMEM((MLA_HEADS * tq, LANE), F32)],
        compiler_params=pltpu.CompilerParams(dimension_semantics=("arbitrary", "arbitrary"),
                                             vmem_limit_bytes=VMEM_LIMIT),
        name="mla_prompt",
    )(qm.reshape(b, l, 256), kmla.reshape(b, l, 256), c.reshape(b, l, 128), lw['wuv'])


def _sample_attn_kernel(ppc, nch, ns, past_len, lam_init, pt_ref, lam_ref,
                        qd_ref, qabs_ref, qr_ref, kn_ref, vn_ref, cn_ref, krn_ref,
                        wuk_ref, ind_ref, ones8_ref, wuv_ref, gsub_ref, ones64_ref, *rest):
    kt_refs = rest[0:ppc]
    vt_refs = rest[ppc:2 * ppc]
    c_refs = rest[2 * ppc:3 * ppc]
    krt_refs = rest[3 * ppc:4 * ppc]
    od_ref, om_ref = rest[4 * ppc:4 * ppc + 2]
    (qd_s, m1, l1, acc1, qa_s, qr_s, m2, l2, acc2, pad_s) = rest[4 * ppc + 2:]
    del pt_ref
    ch = pl.program_id(0) % nch
    nd = 8 * ns
    nm = MLA_HEADS * ns
    scale_d = DIFF_DH ** -0.5
    scale_m = MLA_DQK ** -0.5

    @pl.when(ch == 0)
    def _():
        _fill_diff_q(qd_s, qd_ref[...], ns)
        for h in range(MLA_HEADS):
            qa_s[h * ns:(h + 1) * ns, :] = qabs_ref[:, h * LANE:(h + 1) * LANE].astype(BF16)
            qr_s[h * ns:(h + 1) * ns, :] = qr_ref[:, h * LANE:(h + 1) * LANE].astype(BF16)
        m1[...] = jnp.full(m1.shape, NEG, F32)
        l1[...] = jnp.zeros(l1.shape, F32)
        acc1[...] = jnp.zeros(acc1.shape, F32)
        m2[...] = jnp.full(m2.shape, NEG, F32)
        l2[...] = jnp.zeros(l2.shape, F32)
        acc2[...] = jnp.zeros(acc2.shape, F32)

    rowd = lax.broadcasted_iota(jnp.int32, (nd, 1), 0)
    slope = jnp.zeros((nd, 1), F32)
    for g in range(2):
        for kvh in range(2):
            for mp in range(2):
                r = g * 4 + kvh * 2 + mp
                slope = jnp.where((rowd >= r * ns) & (rowd < (r + 1) * ns), _diff_slope(g, kvh), slope)
    qrow_d = rowd & (ns - 1)
    qrow_m = lax.broadcasted_iota(jnp.int32, (nm, 1), 0) & (ns - 1)

    def head_rows(inv):
        return jnp.concatenate([jnp.broadcast_to(inv[h:h + 1, :], (ns, inv.shape[1])) for h in range(MLA_HEADS)],
                               axis=0)

    sd, sm, cbs = [], [], []
    for j in range(ppc):
        kt = kt_refs[j][...].astype(BF16)
        sd.append(_dot(qd_s[...], kt))
        cb = c_refs[j][...].astype(BF16)
        cbs.append(cb)
        krt = krt_refs[j][...]
        knope = _dot(cb, wuk_ref[...])
        hi, lo = _split2(knope * knope)
        ssq = _dot_nt(ind_ref[...], hi) + _dot_nt(ind_ref[...], lo)
        krsq = jnp.sum(krt * krt, axis=0, keepdims=True)
        inv = lax.rsqrt((ssq + krsq) * (1.0 / MLA_DQK) + EPS)
        num = _dot_nt(qa_s[...], cb) + _dot(qr_s[:, 0:MLA_ROPE], krt.astype(BF16))
        sm.append(num * head_rows(inv) * scale_m)
    n = ppc * PAGE
    kpos = ch * n + lax.broadcasted_iota(jnp.int32, (1, n), 1)
    rel = (kpos - (past_len + qrow_d)).astype(F32)
    s_d = jnp.concatenate(sd, axis=1) * scale_d + slope * rel

    def pv_d(p):
        o = jnp.zeros((nd, LANE), F32)
        for j in range(ppc):
            o = o + _dot_nt(p[:, j * PAGE:(j + 1) * PAGE], vt_refs[j][...].astype(BF16))
        return o

    def pv_m(p):
        o = jnp.zeros((nm, LANE), F32)
        for j in range(ppc):
            o = o + _dot(p[:, j * PAGE:(j + 1) * PAGE], cbs[j])
        return o

    _softmax_update(s_d, m1, l1, acc1, 0, nd, pv_d)
    _softmax_update(jnp.concatenate(sm, axis=1), m2, l2, acc2, 0, nm, pv_m)

    @pl.when(ch == nch - 1)
    def _():
        jpos = lax.broadcasted_iota(jnp.int32, (1, PAGE), 1)
        pad_s[...] = jnp.zeros(pad_s.shape, F32)
        pad_s[0, 0:ns, :] = kn_ref[...]
        pad_s[1, 0:ns, :] = vn_ref[...]
        pad_s[2, 0:ns, :] = cn_ref[...]
        pad_s[3, 0:ns, :] = krn_ref[...]
        knb = pad_s[0].astype(BF16)
        vnb = pad_s[1].astype(BF16)
        cnb = pad_s[2].astype(BF16)
        krn = pad_s[3]
        s = _dot_nt(qd_s[...], knb) * scale_d + slope * (jpos - qrow_d).astype(F32)
        s = jnp.where(jpos <= qrow_d, s, NEG)
        _softmax_update(s, m1, l1, acc1, 0, nd, lambda p: _dot(p, vnb))
        knope = _dot(cnb, wuk_ref[...])
        hi, lo = _split2(knope * knope)
        ssq = _dot_nt(ind_ref[...], hi) + _dot_nt(ind_ref[...], lo)
        hi, lo = _split2(krn * krn)
        krsq = _dot_nt(ones8_ref[...], hi) + _dot_nt(ones8_ref[...], lo)
        inv = lax.rsqrt((ssq + krsq) * (1.0 / MLA_DQK) + EPS)
        num = _dot_nt(qa_s[...], cnb) + _dot_nt(qr_s[...], krn.astype(BF16))
        s = jnp.where(jpos <= qrow_m, num * head_rows(inv) * scale_m, NEG)
        _softmax_update(s, m2, l2, acc2, 0, nm, lambda p: _dot(p, cnb))

        outs = _diff_finish(acc1, l1, lam_ref[0], gsub_ref[...], ones64_ref[...], lam_init, ns)
        od_ref[:, 0:LANE] = outs[0]
        od_ref[:, LANE:2 * LANE] = outs[1]
        om_ref[...] = _mla_finish(acc2, l2, wuv_ref, ns)


def _sample_attn(layer, lam, lw, lam_init, page_table, caches, qd, qabs, qr, kn, vn, cn, krn, nb, ns):
    ckt, cvt, cc, ckrt = caches
    n_pages = page_table.shape[1]
    ppc = math.gcd(n_pages, 8)
    assert ns & (ns - 1) == 0
    nch = n_pages // ppc
    past_len = n_pages * PAGE
    pt_flat = page_table.reshape(-1)

    def page_spec(rows, j):
        return pl.BlockSpec((None, None, rows, PAGE),
                            lambda i, pt: (layer, pt[(i // nch) * n_pages + (i % nch) * ppc + j], 0, 0))

    seq = lambda w: pl.BlockSpec((ns, w), lambda i, pt: (i // nch, 0))
    const = lambda shape: pl.BlockSpec(shape, lambda i, pt: (0,) * len(shape))
    in_specs = [pl.BlockSpec(memory_space=pltpu.SMEM),
                seq(256), seq(512), seq(512), seq(128), seq(128), seq(128), seq(128),
                const((128, 128)), const((8, 128)), const((8, 128)), const((MLA_HEADS, 128, 256)),
                const((1, 128)), const((128, 128))]
    for rows in (PAGE, PAGE, PAGE, MLA_ROPE):
        in_specs += [page_spec(rows, j) for j in range(ppc)]
    ins = [lam, qd, qabs, qr, kn, vn, cn, krn, lw['wuk2d'], lw['ind4'], lw['ones8'], lw['wuv'], lw['gsub'],
           lw['ones64_128']] + [ckt] * ppc + [cvt] * ppc + [cc] * ppc + [ckrt] * ppc
    nd, nm = 8 * ns, MLA_HEADS * ns
    kern = functools.partial(_sample_attn_kernel, ppc, nch, ns, past_len, lam_init)
    return pl.pallas_call(
        kern,
        grid_spec=pltpu.PrefetchScalarGridSpec(
            num_scalar_prefetch=1,
            grid=(nb * nch,),
            in_specs=in_specs,
            out_specs=[seq(256), seq(256)],
            scratch_shapes=[pltpu.VMEM((nd, LANE), BF16), pltpu.VMEM((nd, 1), F32), pltpu.VMEM((nd, 1), F32),
                            pltpu.VMEM((nd, LANE), F32), pltpu.VMEM((nm, LANE), BF16), pltpu.VMEM((nm, LANE), BF16),
                            pltpu.VMEM((nm, 1), F32), pltpu.VMEM((nm, 1), F32), pltpu.VMEM((nm, LANE), F32),
                            pltpu.VMEM((4, PAGE, LANE), F32)]),
        out_shape=[jax.ShapeDtypeStruct((nb * ns, 256), F32), jax.ShapeDtypeStruct((nb * ns, 256), F32)],
        compiler_params=pltpu.CompilerParams(dimension_semantics=("arbitrary",), vmem_limit_bytes=VMEM_LIMIT),
        name="sample_attn",
    )(pt_flat, *ins)


N_ST = 2 * SSM_W // LANE


def _s5_kernel(tb, u_ref, h0re_ref, h0im_ref, bbar_ref, are_ref, aim_ref, cmat_ref, d_ref, wglu_ref,
               o_ref, hre_ref, him_ref, bus, hs, hst):
    tblk = pl.program_id(1)
    half = N_ST // 2

    @pl.when(tblk == 0)
    def _():
        for j in range(half):
            hst[j] = h0re_ref[:, j * LANE:(j + 1) * LANE]
            hst[half + j] = h0im_ref[:, j * LANE:(j + 1) * LANE]

    rows = SUB * tb
    u = u_ref[...].reshape(rows, BRANCH_W)
    bu = _dot(u.astype(BF16), bbar_ref[...])
    for j in range(N_ST):
        bus[j] = bu[:, j * LANE:(j + 1) * LANE]
    ar = [jnp.broadcast_to(are_ref[:, j * LANE:(j + 1) * LANE], (SUB, LANE)) for j in range(half)]
    ai = [jnp.broadcast_to(aim_ref[:, j * LANE:(j + 1) * LANE], (SUB, LANE)) for j in range(half)]

    def step(t, st):
        new = [None] * N_ST
        for j in range(half):
            hr, hi = st[j], st[half + j]
            nr = ar[j] * hr - ai[j] * hi + bus[j, pl.ds(t, SUB, stride=tb), :]
            ni = ar[j] * hi + ai[j] * hr + bus[half + j, pl.ds(t, SUB, stride=tb), :]
            hs[j, pl.ds(t, SUB, stride=tb), :] = nr
            hs[half + j, pl.ds(t, SUB, stride=tb), :] = ni
            new[j], new[half + j] = nr, ni
        return tuple(new)

    st = lax.fori_loop(0, tb, step, tuple(hst[j] for j in range(N_ST)))
    for j in range(N_ST):
        hst[j] = st[j]

    y = d_ref[...] * u
    for j in range(N_ST):
        y = y + _dot(hs[j].astype(BF16), cmat_ref[j * LANE:(j + 1) * LANE, :])
    yg = _dot(y.astype(BF16), wglu_ref[...])
    o_ref[...] = (yg[:, :BRANCH_W] * _sigmoid(yg[:, BRANCH_W:])).reshape(SUB, tb, BRANCH_W)

    @pl.when(tblk == pl.num_programs(1) - 1)
    def _():
        for j in range(half):
            hre_ref[:, j * LANE:(j + 1) * LANE] = st[j]
            him_ref[:, j * LANE:(j + 1) * LANE] = st[half + j]


def _s5(su, h0re, h0im, lw, b, l):
    assert b % SUB == 0
    tb = min(64, l)
    rows = SUB * tb
    const = lambda shape: pl.BlockSpec(shape, lambda bi, ti: (0,) * len(shape))
    st_spec = pl.BlockSpec((SUB, SSM_W), lambda bi, ti: (bi, 0))
    return pl.pallas_call(
        functools.partial(_s5_kernel, tb),
        grid=(b // SUB, l // tb),
        in_specs=[pl.BlockSpec((SUB, tb, BRANCH_W), lambda bi, ti: (bi, ti, 0)), st_spec, st_spec,
                  const((BRANCH_W, 2 * SSM_W)), const((1, SSM_W)), const((1, SSM_W)),
                  const((2 * SSM_W, BRANCH_W)), const((1, BRANCH_W)), const((BRANCH_W, 2 * BRANCH_W))],
        out_specs=[pl.BlockSpec((SUB, tb, BRANCH_W), lambda bi, ti: (bi, ti, 0)), st_spec, st_spec],
        out_shape=[jax.ShapeDtypeStruct((b, l, BRANCH_W), F32), jax.ShapeDtypeStruct((b, SSM_W), F32),
                   jax.ShapeDtypeStruct((b, SSM_W), F32)],
        scratch_shapes=[pltpu.VMEM((N_ST, rows, LANE), F32), pltpu.VMEM((N_ST, rows, LANE), F32),
                        pltpu.VMEM((N_ST, SUB, LANE), F32)],
        compiler_params=pltpu.CompilerParams(dimension_semantics=("arbitrary", "arbitrary"),
                                             vmem_limit_bytes=VMEM_LIMIT),
        name="s5",
    )(su.reshape(b, l, BRANCH_W), h0re, h0im, lw['bbar'], lw['a_re'], lw['a_im'], lw['cmat'], lw['ssm_d'], lw['wglu'])


def _hgrn_kernel(chunk, nchunk, chain, q_ref, f_ref, v_ref, s0_ref, lb_ref, gain_ref, tri_ref, same_ref,
                 ones64_ref, o_ref, sfin_ref, kp, bcp, vp, st, oacc):
    rows = chunk * nchunk
    q = q_ref[0]
    zf = f_ref[0]
    v = v_ref[0]
    lb = lb_ref[...]
    logf = jnp.log(lb + (1.0 - lb) * _sigmoid(zf))
    kk = (1.0 - lb) * _sigmoid(-zf)
    bc = _dot3t(tri_ref[...], logf)
    bl = _dot3t(same_ref[...], logf)
    qh = q * jnp.exp(bc)
    kdec = (kk * jnp.exp(bl - bc)).astype(BF16)

    kp[0:HG_CHUNK, :] = jnp.zeros((HG_CHUNK, BRANCH_W), F32)
    bcp[0:HG_CHUNK, :] = jnp.zeros((HG_CHUNK, BRANCH_W), F32)
    vp[0:HG_CHUNK, :] = jnp.zeros((HG_CHUNK, BRANCH_W), F32)
    kp[HG_CHUNK:HG_CHUNK + rows, :] = kk
    bcp[HG_CHUNK:HG_CHUNK + rows, :] = bc
    vp[HG_CHUNK:HG_CHUNK + rows, :] = v
    rpos = lax.broadcasted_iota(jnp.int32, (rows, 1), 0) & (chunk - 1)
    o = jnp.zeros((rows, BRANCH_W), F32)
    for d in range(chunk):
        lo = HG_CHUNK - d
        ks = kp[lo:lo + rows, :]
        bcs = bcp[lo:lo + rows, :]
        vs = vp[lo:lo + rows, :]
        x = q * ks * jnp.exp(jnp.where(rpos >= d, bc - bcs, NEG))
        o = o + _dot2(x, ones64_ref[...]) * vs
    oacc[...] = o

    lane = lax.broadcasted_iota(jnp.int32, (1, BRANCH_W), 1)
    rowi = lax.broadcasted_iota(jnp.int32, (rows, 1), 0)
    if chain:
        @pl.when(pl.program_id(1) == 0)
        def _():
            st[...] = s0_ref[0]
    for i in range(nchunk):
        r0 = i * chunk
        s_t = st[...] if chain else s0_ref[i]
        full = jnp.concatenate(
            [jnp.where((lane >= h * HG_DK) & (lane < (h + 1) * HG_DK), s_t, 0.0) for h in range(HG_HEADS)],
            axis=0).astype(BF16)
        oacc[r0:r0 + chunk, :] = oacc[r0:r0 + chunk, :] + _dot_nt(qh[r0:r0 + chunk, :].astype(BF16), full)
        vmask = jnp.where((rowi >= r0) & (rowi < r0 + chunk), v, 0.0).astype(BF16)
        kvt = _dot_tn(vmask, kdec)
        comp = jnp.zeros((HG_DV, BRANCH_W), F32)
        for h in range(HG_HEADS):
            comp = jnp.where((lane >= h * HG_DK) & (lane < (h + 1) * HG_DK), kvt[h * HG_DV:(h + 1) * HG_DV, :], comp)
        s_new = s_t * jnp.exp(bl[r0:r0 + 1, :]) + comp
        if chain:
            st[...] = s_new
        else:
            sfin_ref[i] = s_new
    if chain:
        sfin_ref[0] = st[...]
    ot = oacc[...]
    msq = _dot2(ot * ot, ones64_ref[...]) * (1.0 / HG_DV)
    o_ref[0] = ot * lax.rsqrt(msq + EPS) * gain_ref[...]


def _dot3t(m, x):
    hi, mid, lo = _split3(x)
    return _dot(m, hi) + _dot(m, mid) + _dot(m, lo)


def _hgrn(hq, hf, hi, s0t, lw, b, l):
    chunk = math.gcd(l, HG_CHUNK)
    if l >= PAGE:
        chain, rows, nblk, outer = True, PAGE, l // PAGE, b
    else:
        assert l == chunk
        per = min(b, PAGE // l)
        chain, rows, nblk, outer = False, per * l, 1, b // per
    nchunk = rows // chunk
    sblk = 1 if chain else nchunk
    r = np.arange(rows)
    same = (r[:, None] // chunk) == (r[None, :] // chunk)
    tri = same & (r[None, :] <= r[:, None])
    blk = lambda: pl.BlockSpec((1, rows, BRANCH_W), lambda bi, ti: (bi * nblk + ti, 0, 0))
    const = lambda shape: pl.BlockSpec(shape, lambda bi, ti: (0,) * len(shape))
    st_spec = pl.BlockSpec((sblk, HG_DV, BRANCH_W), lambda bi, ti: (bi, 0, 0))
    shp = (outer * nblk, rows, BRANCH_W)
    o, sfin = pl.pallas_call(
        functools.partial(_hgrn_kernel, chunk, nchunk, chain),
        grid=(outer, nblk),
        in_specs=[blk(), blk(), blk(), st_spec, const((1, BRANCH_W)), const((1, BRANCH_W)),
                  const((rows, rows)), const((rows, rows)), const((BRANCH_W, BRANCH_W))],
        out_specs=[blk(), st_spec],
        out_shape=[jax.ShapeDtypeStruct(shp, F32), jax.ShapeDtypeStruct((b, HG_DV, BRANCH_W), F32)],
        scratch_shapes=[pltpu.VMEM((HG_CHUNK + rows, BRANCH_W), F32), pltpu.VMEM((HG_CHUNK + rows, BRANCH_W), F32),
                        pltpu.VMEM((HG_CHUNK + rows, BRANCH_W), F32), pltpu.VMEM((HG_DV, BRANCH_W), F32),
                        pltpu.VMEM((rows, BRANCH_W), F32)],
        compiler_params=pltpu.CompilerParams(dimension_semantics=("arbitrary", "arbitrary"),
                                             vmem_limit_bytes=VMEM_LIMIT),
        name="hgrn_chain" if chain else "hgrn_step",
    )(hq.reshape(shp), hf.reshape(shp), hi.reshape(shp), s0t, lw['lb'], lw['hg_gain'],
      jnp.asarray(tri, BF16), jnp.asarray(same, BF16), lw['ones64'])
    return o.reshape(b * l, BRANCH_W), sfin


def _block_ones(n, blk):
    r = np.arange(n)
    return jnp.asarray((r[:, None] // blk) == (r[None, :] // blk), BF16)


def _layout_indices():
    zero = IN_COLS
    a = np.full((NA,), zero, np.int64)
    for g in range(2):
        for kvh in range(2):
            for mp in range(2):
                dst = _A['dq'] + g * 128 + kvh * 64 + mp * 32
                src = _OFF['dq'] + kvh * 128 + g * 64 + mp * 32
                a[dst:dst + 32] = np.arange(src, src + 32)
    a[_A['dk']:_A['dk'] + 128] = np.arange(_OFF['dk'], _OFF['dk'] + 128)
    a[_A['dv']:_A['dv'] + 128] = np.arange(_OFF['dv'], _OFF['dv'] + 128)
    a[_A['mqa']:_A['mqa'] + MLA_QL] = np.arange(_OFF['mqa'], _OFF['mqa'] + MLA_QL)
    a[_A['mkva']:_A['mkva'] + 128] = np.arange(_OFF['mkva'], _OFF['mkva'] + 128)
    for h in range(MLA_HEADS):
        dst = _A['krt'] + h * 64 + MLA_NOPE
        a[dst:dst + MLA_ROPE] = np.arange(_OFF['mkr'], _OFF['mkr'] + MLA_ROPE)
    a[_A['kr0']:_A['kr0'] + MLA_ROPE] = np.arange(_OFF['mkr'], _OFF['mkr'] + MLA_ROPE)
    for name in ('su', 'hq', 'hf', 'hi'):
        a[_A[name]:_A[name] + 256] = np.arange(_OFF[name], _OFF[name] + 256)
    perm = np.zeros((256,), np.int64)
    for g in range(2):
        for kvh in range(2):
            perm[g * 128 + kvh * 64:g * 128 + kvh * 64 + 64] = np.arange(kvh * 128 + g * 64, kvh * 128 + g * 64 + 64)
    bcols = np.concatenate([_OFF['gates'] + perm, np.arange(_OFF['gates'] + 256, _OFF['gates'] + 1024),
                            np.arange(_OFF['merge'], _OFF['merge'] + N_BRANCH * D_MODEL)])
    return a, bcols, perm


_IDX_A, _IDX_B, _PERM_DIFF = _layout_indices()


def _head48(vec):
    return jnp.tile(jnp.pad(vec, (0, 64 - MLA_DQK)), MLA_HEADS).reshape(1, 256)


def _prep_layer(p, l, lb_all):
    w_in = jnp.pad(p['w_in'][l], ((0, 0), (0, 1)))
    lw = {}
    lw['g'] = p['norm_gain'][l].reshape(1, D_MODEL)
    lw['wa'] = jnp.take(w_in, _IDX_A, axis=1).astype(BF16)
    lw['wb'] = jnp.take(w_in, _IDX_B, axis=1).astype(BF16)
    wbr = p['w_branch'][l]
    lw['wbr'] = jnp.concatenate([jnp.take(wbr[0], _PERM_DIFF, axis=0)[None], wbr[1:]], axis=0).astype(BF16)
    lw['wout'] = p['w_out'][l].astype(BF16)
    lw['gq'] = jnp.tile(p['diff_q_gain'][l], 8).reshape(1, 256)
    lw['gk'] = jnp.tile(p['diff_k_gain'][l], 4).reshape(1, 128)
    lw['gsub'] = jnp.tile(p['diff_subln_gain'][l], 2).reshape(1, 128)
    lw['gqa'] = jnp.pad(p['mla_qa_gain'][l], (0, 256 - MLA_QL)).reshape(1, 256)
    wuq = p['w_mla_uq'][l].reshape(MLA_QL, MLA_HEADS, MLA_DQK)
    lw['wuq'] = jnp.pad(wuq, ((0, 256 - MLA_QL), (0, 0), (0, 64 - MLA_DQK))).reshape(256, 256).astype(BF16)
    lw['gq48'] = _head48(p['mla_q_gain'][l])
    lw['gk48'] = _head48(p['mla_k_gain'][l])
    lw['gkva'] = p['mla_kva_gain'][l].reshape(1, 128)
    wuk = p['w_mla_uk'][l]
    lw['wukp'] = jnp.pad(wuk, ((0, 0), (0, 0), (0, 64 - MLA_NOPE))).reshape(128, 256).astype(BF16)
    lw['wuk2d'] = wuk.reshape(128, MLA_HEADS * MLA_NOPE).astype(BF16)
    wabs = jnp.zeros((MLA_HEADS, 64, MLA_HEADS, 128), F32)
    selr = np.zeros((MLA_HEADS, 64, MLA_HEADS, 128), np.float32)
    for h in range(MLA_HEADS):
        wabs = wabs.at[h, :MLA_NOPE, h, :].set(wuk[:, h, :].T)
        selr[h, MLA_NOPE + np.arange(MLA_ROPE), h, np.arange(MLA_ROPE)] = 1.0
    lw['wabs'] = wabs.reshape(256, 512).astype(BF16)
    lw['selr'] = jnp.asarray(selr.reshape(256, 512), BF16)
    wuv = p['w_mla_uv'][l]
    wuvp = jnp.zeros((MLA_HEADS, 128, MLA_HEADS, MLA_DV), F32)
    for h in range(MLA_HEADS):
        wuvp = wuvp.at[h, :, h, :].set(wuv[:, h, :])
    lw['wuv'] = wuvp.reshape(MLA_HEADS, 128, 256).astype(BF16)
    ind = np.zeros((8, 128), np.float32)
    for h in range(MLA_HEADS):
        ind[h, h * MLA_NOPE:(h + 1) * MLA_NOPE] = 1.0
    lw['ind4'] = jnp.asarray(ind, BF16)
    ones8 = np.zeros((8, 128), np.float32)
    ones8[:, :MLA_ROPE] = 1.0
    lw['ones8'] = jnp.asarray(ones8, BF16)
    lw['ones32'] = _block_ones(256, 32)
    lw['ones64'] = _block_ones(256, 64)
    lw['ones64_128'] = _block_ones(128, 64)

    lam = lax.complex(p['ssm_a_re'][l], p['ssm_a_im'][l])
    dt = jnp.exp(p['ssm_log_dt'][l])[:, None]
    a_bar = jnp.exp(lam * dt)
    bmat = lax.complex(p['ssm_b_re'][l], p['ssm_b_im'][l])
    b_bar = ((a_bar - 1.0) / lam)[..., None] * bmat
    cmat = lax.complex(p['ssm_c_re'][l], p['ssm_c_im'][l])
    eye = jnp.eye(SSM_GROUPS, dtype=F32)
    bb = jnp.transpose(b_bar, (0, 2, 1))
    b_re = jnp.einsum('ghp,gk->ghkp', jnp.real(bb), eye).reshape(BRANCH_W, SSM_W)
    b_im = jnp.einsum('ghp,gk->ghkp', jnp.imag(bb), eye).reshape(BRANCH_W, SSM_W)
    lw['bbar'] = jnp.concatenate([b_re, b_im], axis=1).astype(BF16)
    cc = jnp.transpose(cmat, (0, 2, 1))
    c_re = jnp.einsum('gph,gk->gpkh', jnp.real(cc), eye).reshape(SSM_W, BRANCH_W)
    c_im = jnp.einsum('gph,gk->gpkh', jnp.imag(cc), eye).reshape(SSM_W, BRANCH_W)
    lw['cmat'] = jnp.concatenate([c_re, -c_im], axis=0).astype(BF16)
    lw['a_re'] = jnp.real(a_bar).reshape(1, SSM_W)
    lw['a_im'] = jnp.imag(a_bar).reshape(1, SSM_W)
    lw['ssm_d'] = p['ssm_d'][l].reshape(1, BRANCH_W)
    lw['wglu'] = p['w_glu'][l].astype(BF16)
    lw['lb'] = lb_all[l].reshape(1, BRANCH_W)
    lw['hg_gain'] = jnp.tile(p['hg_norm_gain'][l], HG_HEADS).reshape(1, BRANCH_W)
    lp = p['diff_lambda'][l]
    lam_init = 0.8 - 0.6 * math.exp(-0.3 * l)
    lw['lam'] = (jnp.exp(jnp.sum(lp[0] * lp[1])) - jnp.exp(jnp.sum(lp[2] * lp[3])) + lam_init).reshape(1)
    return lw, lam_init


def _rope_tables(pos):
    half = MLA_ROPE // 2
    freqs = ROPE_BASE ** (-jnp.arange(half, dtype=F32) / half)
    ang = pos.astype(F32)[:, None] * freqs
    cos, sin = jnp.cos(ang), jnp.sin(ang)
    n = pos.shape[0]
    one = jnp.ones((n, MLA_NOPE), F32)
    zero = jnp.zeros((n, MLA_NOPE), F32)
    pad1 = jnp.ones((n, 64 - MLA_DQK), F32)
    pad0 = jnp.zeros((n, 64 - MLA_DQK), F32)
    cs_h = jnp.concatenate([one, cos, cos, pad1], axis=1)
    sn_h = jnp.concatenate([zero, -sin, sin, pad0], axis=1)
    cs0 = jnp.concatenate([cos, cos, jnp.ones((n, LANE - MLA_ROPE), F32)], axis=1)
    sn0 = jnp.concatenate([-sin, sin, jnp.zeros((n, LANE - MLA_ROPE), F32)], axis=1)
    return dict(cs=jnp.tile(cs_h, (1, 2)), sn=jnp.tile(sn_h, (1, 2)), cs0=cs0, sn0=sn0)


def kernel(x_prompt, x_sample, cache_diff_k, cache_diff_v, cache_mla_c, cache_mla_kr, state_ssm_re, state_ssm_im, state_hgrn, page_table, norm_gain, w_in, w_branch, w_out, diff_q_gain, diff_k_gain, diff_lambda, diff_subln_gain, mla_qa_gain, mla_kva_gain, w_mla_uq, w_mla_uk, w_mla_uv, mla_q_gain, mla_k_gain, ssm_a_re, ssm_a_im, ssm_log_dt, ssm_b_re, ssm_b_im, ssm_c_re, ssm_c_im, ssm_d, w_glu, hg_lb_logits, hg_norm_gain):
    p = dict(norm_gain=norm_gain, w_in=w_in, w_branch=w_branch, w_out=w_out, diff_q_gain=diff_q_gain,
             diff_k_gain=diff_k_gain, diff_lambda=diff_lambda, diff_subln_gain=diff_subln_gain,
             mla_qa_gain=mla_qa_gain, mla_kva_gain=mla_kva_gain, w_mla_uq=w_mla_uq, w_mla_uk=w_mla_uk,
             w_mla_uv=w_mla_uv, mla_q_gain=mla_q_gain, mla_k_gain=mla_k_gain, ssm_a_re=ssm_a_re, ssm_a_im=ssm_a_im,
             ssm_log_dt=ssm_log_dt, ssm_b_re=ssm_b_re, ssm_b_im=ssm_b_im, ssm_c_re=ssm_c_re, ssm_c_im=ssm_c_im,
             ssm_d=ssm_d, w_glu=w_glu, hg_norm_gain=hg_norm_gain)
    depth = w_in.shape[0]
    bp, lp, _ = x_prompt.shape
    bs, ls, _ = x_sample.shape
    n_pages = page_table.shape[1]
    past_len = n_pages * PAGE
    tp, ts = bp * lp, bs * ls

    sm = jax.nn.softmax(hg_lb_logits.astype(F32), axis=0)
    lb_all = jnp.cumsum(sm, axis=0) - sm[0]

    tm_p = min(512, lp)
    tm_s = min(512, ts)
    assert tm_s % ls == 0
    tabs_p = _rope_tables(jnp.arange(lp))
    tabs_s = _rope_tables(past_len + (jnp.arange(tm_s) % ls))

    n_pool = cache_diff_k.shape[1]
    ckt = jnp.transpose(cache_diff_k, (0, 1, 3, 4, 2)).reshape(depth, n_pool, 128, PAGE)
    cvt = jnp.transpose(cache_diff_v, (0, 1, 3, 4, 2)).reshape(depth, n_pool, 128, PAGE)
    ckrt = jnp.transpose(cache_mla_kr, (0, 1, 3, 2))
    caches = (ckt, cvt, cache_mla_c, ckrt)

    hp = x_prompt.reshape(tp, D_MODEL)
    hs = x_sample.reshape(ts, D_MODEL)
    zeros_p = jnp.zeros((bp, SSM_W), F32)
    zeros_st = jnp.zeros((bp, HG_DV, BRANCH_W), F32)
    outs_p, outs_s = [], []
    for l in range(depth):
        lw, lam_init = _prep_layer(p, l, lb_all)
        (qd, kd, vd, qm, c, kmla, kr16, su, hq, hf, hi) = _inproj(hp, lw, tabs_p, tm_p, lp // tm_p, False)
        o_diff = _diff_prompt(qd, kd, vd, lw['lam'], lw, lam_init, bp, lp).reshape(tp, 256)
        o_mla = _mla_prompt(qm, kmla, c, lw, bp, lp).reshape(tp, 256)
        o_ssm, hre, him = _s5(su, zeros_p, zeros_p, lw, bp, lp)
        o_hg, sfin = _hgrn(hq, hf, hi, zeros_st, lw, bp, lp)
        hp = _merge(hp, lw, (o_diff, o_mla, o_ssm.reshape(tp, 256), o_hg), tm_p)
        outs_p.append((kd, vd, c, kr16, hre, him, sfin))
        (qd, kd, vd, c, kr16, kr128, qabs, qr, su, hq, hf, hi) = _inproj(hs, lw, tabs_s, tm_s, 1, True)
        o_diff, o_mla = _sample_attn(l, lw['lam'], lw, lam_init, page_table, caches,
                                     qd, qabs, qr, kd, vd, c, kr128, bs, ls)
        o_ssm, hre, him = _s5(su, state_ssm_re[l].reshape(bs, SSM_W), state_ssm_im[l].reshape(bs, SSM_W), lw, bs, ls)
        s0t = jnp.transpose(state_hgrn[l], (0, 3, 1, 2)).reshape(bs, HG_DV, BRANCH_W)
        o_hg, sfin = _hgrn(hq, hf, hi, s0t, lw, bs, ls)
        hs = _merge(hs, lw, (o_diff, o_mla, o_ssm.reshape(ts, 256), o_hg), tm_s)
        outs_s.append((kd, vd, c, kr16, hre, him, sfin))

    def stack(group, i):
        return jnp.stack([st[i] for st in group], axis=0)

    def unstate(s, b):
        return jnp.transpose(s.reshape(depth, b, HG_DV, HG_HEADS, HG_DK), (0, 1, 3, 4, 2))

    npg = lp // PAGE
    return (hp.reshape(bp, lp, D_MODEL), hs.reshape(bs, ls, D_MODEL),
            stack(outs_p, 0).reshape(depth, bp, npg, PAGE, DIFF_KVH, 2 * DIFF_DH),
            stack(outs_p, 1).reshape(depth, bp, npg, PAGE, DIFF_KVH, DIFF_DV),
            stack(outs_p, 2).reshape(depth, bp, npg, PAGE, MLA_KVL),
            stack(outs_p, 3).reshape(depth, bp, npg, PAGE, MLA_ROPE),
            stack(outs_p, 4).reshape(depth, bp, SSM_GROUPS, SSM_STATE),
            stack(outs_p, 5).reshape(depth, bp, SSM_GROUPS, SSM_STATE),
            unstate(stack(outs_p, 6), bp),
            stack(outs_s, 0).reshape(depth, bs, ls, DIFF_KVH, 2 * DIFF_DH),
            stack(outs_s, 1).reshape(depth, bs, ls, DIFF_KVH, DIFF_DV),
            stack(outs_s, 2).reshape(depth, bs, ls, MLA_KVL),
            stack(outs_s, 3).reshape(depth, bs, ls, MLA_ROPE),
            stack(outs_s, 4).reshape(depth, bs, SSM_GROUPS, SSM_STATE),
            stack(outs_s, 5).reshape(depth, bs, SSM_GROUPS, SSM_STATE),
            unstate(stack(outs_s, 6), bs))
```

```python
import functools
import math

import numpy as np
import jax
import jax.numpy as jnp
from jax import lax
from jax.experimental import pallas as pl
from jax.experimental.pallas import tpu as pltpu

F32 = jnp.float32
BF16 = jnp.bfloat16
EPS = 1e-6
NEG = -1e30

D_MODEL = 1024
BRANCH_W = 256
N_BRANCH = 4
DIFF_HEADS, DIFF_KVH, DIFF_DH, DIFF_DV = 4, 2, 32, 64
MLA_HEADS, MLA_QL, MLA_KVL, MLA_NOPE, MLA_ROPE, MLA_DV = 4, 192, 128, 32, 16, 64
MLA_DQK = MLA_NOPE + MLA_ROPE
ROPE_BASE = 10000.0
SSM_GROUP, SSM_GROUPS, SSM_STATE = 16, 16, 64
SSM_W = SSM_GROUPS * SSM_STATE
HG_HEADS, HG_DK, HG_DV, HG_CHUNK = 4, 64, 64, 16
PAGE = 128
LANE = 128
SUB = 8
VMEM_LIMIT = 56 * 1024 * 1024

_OFF = dict(dq=0, dk=256, dv=384, mqa=512, mkva=704, mkr=832, su=848, hq=1104, hf=1360, hi=1616,
            gates=1872, merge=2896)
IN_COLS = 6992
_A = dict(dq=0, dk=256, dv=384, mqa=512, mkva=768, krt=896, kr0=1152, su=1280, hq=1536, hf=1792, hi=2048)
NA = 2304


def _dot(a, b):
    return jnp.dot(a, b, preferred_element_type=F32)


def _dot_nt(a, b):
    return lax.dot_general(a, b, (((1,), (1,)), ((), ())), preferred_element_type=F32)


def _dot_tn(a, b):
    return lax.dot_general(a, b, (((0,), (0,)), ((), ())), preferred_element_type=F32)


def _split2(x):
    hi = x.astype(BF16)
    lo = (x - hi.astype(F32)).astype(BF16)
    return hi, lo


def _split3(x):
    hi = x.astype(BF16)
    r = x - hi.astype(F32)
    mid = r.astype(BF16)
    lo = (r - mid.astype(F32)).astype(BF16)
    return hi, mid, lo


def _dot2(x, w):
    hi, lo = _split2(x)
    return _dot(hi, w) + _dot(lo, w)


def _dot3(x, w):
    hi, mid, lo = _split3(x)
    return _dot(hi, w) + _dot(mid, w) + _dot(lo, w)


def _sigmoid(x):
    return 1.0 / (1.0 + jnp.exp(-x))


def _rope_lanes(x, cs, sn, x1mask):
    rot = jnp.where(x1mask, pltpu.roll(x, LANE - MLA_ROPE // 2, 1), pltpu.roll(x, MLA_ROPE // 2, 1))
    return x * cs + rot * sn


def _inproj_kernel(sample, *refs):
    (x_ref, g_ref, wa_ref, gq_ref, gk_ref, gqa_ref, wuq_ref, gq48_ref, gkva_ref, wukp_ref, gk48_ref,
     cs_ref, sn_ref, cs0_ref, sn0_ref, ones32_ref, ones64_ref) = refs[:17]
    if sample:
        wabs_ref, selr_ref = refs[17:19]
        (qd_ref, kd_ref, vd_ref, c_ref, kr16_ref, kr128_ref, qabs_ref, qr_ref,
         su_ref, hq_ref, hf_ref, hi_ref) = refs[19:]
    else:
        (qd_ref, kd_ref, vd_ref, qm_ref, c_ref, kmla_ref, kr16_ref,
         su_ref, hq_ref, hf_ref, hi_ref) = refs[17:]

    x = x_ref[...]
    ms = jnp.mean(x * x, axis=-1, keepdims=True)
    xn = (x * lax.rsqrt(ms + EPS) * g_ref[...]).astype(BF16)
    h = _dot(xn, wa_ref[...])

    def seg(name, width):
        return h[:, _A[name]:_A[name] + width]

    dq = seg('dq', 256)
    msq = _dot2(dq * dq, ones32_ref[...]) * (1.0 / DIFF_DH)
    qd_ref[...] = dq * lax.rsqrt(msq + EPS) * gq_ref[...]
    dk = seg('dk', 128)
    msk = _dot2(dk * dk, ones32_ref[0:128, 0:128]) * (1.0 / DIFF_DH)
    kd_ref[...] = dk * lax.rsqrt(msk + EPS) * gk_ref[...]
    vd_ref[...] = seg('dv', 128)

    lane = lax.broadcasted_iota(jnp.int32, (1, LANE), 1)
    x1_head = ((lane & 63) >= MLA_NOPE) & ((lane & 63) < MLA_NOPE + MLA_ROPE // 2)
    x1_zero = lane < MLA_ROPE // 2
    cs = cs_ref[...]
    sn = sn_ref[...]

    mqa = seg('mqa', 256)
    msa = jnp.sum(mqa * mqa, axis=-1, keepdims=True) * (1.0 / MLA_QL)
    qa = (mqa * lax.rsqrt(msa + EPS) * gqa_ref[...]).astype(BF16)
    q = _dot(qa, wuq_ref[...])
    q = jnp.concatenate([_rope_lanes(q[:, :LANE], cs, sn, x1_head),
                         _rope_lanes(q[:, LANE:], cs, sn, x1_head)], axis=1)
    msq2 = _dot2(q * q, ones64_ref[...]) * (1.0 / MLA_DQK)
    qm = q * lax.rsqrt(msq2 + EPS) * gq48_ref[...]

    mkva = seg('mkva', 128)
    msc = jnp.mean(mkva * mkva, axis=-1, keepdims=True)
    c = mkva * lax.rsqrt(msc + EPS) * gkva_ref[...]
    c_ref[...] = c

    kr0 = _rope_lanes(seg('kr0', 128), cs0_ref[...], sn0_ref[...], x1_zero)
    kr16_ref[...] = kr0[:, :MLA_ROPE]

    if sample:
        kr128_ref[...] = kr0
        qg = qm * gk48_ref[...]
        qabs_ref[...] = _dot2(qg, wabs_ref[...])
        qr_ref[...] = _dot3(qg, selr_ref[...])
    else:
        qm_ref[...] = qm
        krt = seg('krt', 256)
        krt = jnp.concatenate([_rope_lanes(krt[:, :LANE], cs, sn, x1_head),
                               _rope_lanes(krt[:, LANE:], cs, sn, x1_head)], axis=1)
        kpre = _dot(c.astype(BF16), wukp_ref[...]) + krt
        msk2 = _dot2(kpre * kpre, ones64_ref[...]) * (1.0 / MLA_DQK)
        kmla_ref[...] = kpre * lax.rsqrt(msk2 + EPS) * gk48_ref[...]

    su_ref[...] = seg('su', 256)
    hq_ref[...] = seg('hq', 256)
    hf_ref[...] = seg('hf', 256)
    hi_ref[...] = seg('hi', 256)


def _inproj(x, lw, tabs, tm, n_tab, sample):
    t = x.shape[0]
    assert t % tm == 0
    const = lambda shape: pl.BlockSpec(shape, lambda i: (0,) * len(shape))
    row = lambda w: pl.BlockSpec((tm, w), lambda i: (i, 0))
    tab = pl.BlockSpec((tm, LANE), lambda i: (i % n_tab, 0))
    ins = [x, lw['g'], lw['wa'], lw['gq'], lw['gk'], lw['gqa'], lw['wuq'], lw['gq48'], lw['gkva'], lw['wukp'],
           lw['gk48'], tabs['cs'], tabs['sn'], tabs['cs0'], tabs['sn0'], lw['ones32'], lw['ones64']]
    in_specs = [row(D_MODEL), const((1, D_MODEL)), const((D_MODEL, NA)), const((1, 256)), const((1, 128)),
                const((1, 256)), const((256, 256)), const((1, 256)), const((1, 128)), const((128, 256)),
                const((1, 256)), tab, tab, tab, tab, const((256, 256)), const((256, 256))]
    if sample:
        ins += [lw['wabs'], lw['selr']]
        in_specs += [const((256, 512)), const((256, 512))]
        widths = [256, 128, 128, 128, MLA_ROPE, 128, 512, 512, 256, 256, 256, 256]
    else:
        widths = [256, 128, 128, 256, 128, 256, MLA_ROPE, 256, 256, 256, 256]
    return pl.pallas_call(
        functools.partial(_inproj_kernel, sample),
        grid=(t // tm,),
        in_specs=in_specs,
        out_specs=[row(w) for w in widths],
        out_shape=[jax.ShapeDtypeStruct((t, w), F32) for w in widths],
        compiler_params=pltpu.CompilerParams(dimension_semantics=("arbitrary",), vmem_limit_bytes=VMEM_LIMIT),
        name="inproj_sample" if sample else "inproj_prompt",
    )(*ins)


def _merge_kernel(x_ref, g_ref, wb_ref, b0_ref, b1_ref, b2_ref, b3_ref, wbr_ref, wout_ref, y_ref):
    x = x_ref[...]
    ms = jnp.mean(x * x, axis=-1, keepdims=True)
    xn = (x * lax.rsqrt(ms + EPS) * g_ref[...]).astype(BF16)
    m = jnp.zeros(x.shape, F32)
    for k, b_ref in enumerate((b0_ref, b1_ref, b2_ref, b3_ref)):
        gates = _dot(xn, wb_ref[:, k * BRANCH_W:(k + 1) * BRANCH_W])
        br = b_ref[...] * (gates * _sigmoid(gates))
        up = _dot(br.astype(BF16), wbr_ref[k])
        mg = _dot(xn, wb_ref[:, N_BRANCH * BRANCH_W + k * D_MODEL:N_BRANCH * BRANCH_W + (k + 1) * D_MODEL])
        m = m + _sigmoid(mg) * up
    y_ref[...] = x + _dot(m.astype(BF16), wout_ref[...])


def _merge(x, lw, branches, tm):
    t = x.shape[0]
    const = lambda shape: pl.BlockSpec(shape, lambda i: (0,) * len(shape))
    row = lambda w: pl.BlockSpec((tm, w), lambda i: (i, 0))
    nb = N_BRANCH * BRANCH_W + N_BRANCH * D_MODEL
    return pl.pallas_call(
        _merge_kernel,
        grid=(t // tm,),
        in_specs=[row(D_MODEL), const((1, D_MODEL)), const((D_MODEL, nb)), row(256), row(256), row(256), row(256),
                  const((N_BRANCH, BRANCH_W, D_MODEL)), const((D_MODEL, D_MODEL))],
        out_specs=row(D_MODEL),
        out_shape=jax.ShapeDtypeStruct((t, D_MODEL), F32),
        compiler_params=pltpu.CompilerParams(dimension_semantics=("arbitrary",), vmem_limit_bytes=VMEM_LIMIT),
        name="merge",
    )(x, lw['g'], lw['wb'], *branches, lw['wbr'], lw['wout'])


def _diff_slope(g, kvh):
    head = kvh * (DIFF_HEADS // DIFF_KVH) + g
    return 2.0 ** (-8.0 * (head + 1) / DIFF_HEADS)


LOG2E = 1.4426950408889634


def _diff_rows():
    return [(kvh, g, mp) for kvh in range(2) for g in range(2) for mp in range(2)]


def _fill_diff_q(qbig_ref, q, rows):
    lane = lax.broadcasted_iota(jnp.int32, (1, LANE), 1)
    for r, (kvh, g, mp) in enumerate(_diff_rows()):
        lo = kvh * 64 + mp * 32
        qbig_ref[r * rows:(r + 1) * rows, :] = jnp.where(
            (lane >= lo) & (lane < lo + DIFF_DH), q[:, g * LANE:(g + 1) * LANE], 0.0).astype(BF16)


def _diff_finish(o_of, lam, gsub, ones64, lam_init):
    lane = lax.broadcasted_iota(jnp.int32, (1, LANE), 1)
    outs = []
    for g in range(2):
        per_kvh = [o_of(kvh * 4 + g * 2) - lam * o_of(kvh * 4 + g * 2 + 1) for kvh in range(2)]
        og = jnp.where(lane < DIFF_DV, per_kvh[0], per_kvh[1])
        msq = _dot2(og * og, ones64) * (1.0 / DIFF_DV)
        outs.append(og * lax.rsqrt(msq + EPS) * gsub * (1.0 - lam_init))
    return outs


def _online_update(t, m_ref, r0, rows):
    m_old = m_ref[r0:r0 + rows, :]
    m_new = jnp.maximum(m_old, jnp.max(t, axis=-1, keepdims=True))
    m_ref[r0:r0 + rows, :] = m_new
    alpha = jnp.exp2(m_old - m_new)
    ps = [jnp.exp2(t[:, c * LANE:(c + 1) * LANE] - m_new) for c in range(t.shape[1] // LANE)]
    return alpha, ps


def _diffp_kernel(tq, tk, lam_init, lam_ref, q_ref, k_ref, v_ref, gsub_ref, ones64_ref, o_ref,
                  qbig_ref, kb_ref, vb_ref, p_ref, m_ref, acc_ref):
    qi = pl.program_id(1)
    lane = lax.broadcasted_iota(jnp.int32, (1, LANE), 1)

    @pl.when(qi == 0)
    def _():
        kb_ref[...] = k_ref[0].astype(BF16)
        v = v_ref[0]
        vb_ref[0] = jnp.where(lane < DIFF_DV, v, 1.0).astype(BF16)
        vb_ref[1] = jnp.where(lane < DIFF_DV, 1.0, v).astype(BF16)

    _fill_diff_q(qbig_ref, q_ref[0], tq)
    m_ref[...] = jnp.full(m_ref.shape, NEG, F32)
    acc_ref[...] = jnp.zeros(acc_ref.shape, F32)
    c1 = DIFF_DH ** -0.5 * LOG2E
    qpos = qi * tq + lax.broadcasted_iota(jnp.int32, (tq, 1), 0)

    def kv_step(j, masked):
        s_all = _dot_nt(qbig_ref[...], kb_ref[pl.ds(j * tk, tk), :])
        kpos = j * tk + lax.broadcasted_iota(jnp.int32, (1, tk), 1)
        kposf = kpos.astype(F32)
        for r, (kvh, g, mp) in enumerate(_diff_rows()):
            t = s_all[r * tq:(r + 1) * tq, :] * c1 + (_diff_slope(g, kvh) * LOG2E) * kposf
            if masked:
                t = jnp.where(kpos <= qpos, t, NEG)
            alpha, ps = _online_update(t, m_ref, r * tq, tq)
            acc_ref[r * tq:(r + 1) * tq, :] = alpha * acc_ref[r * tq:(r + 1) * tq, :]
            for c, p in enumerate(ps):
                p_ref[r * tq:(r + 1) * tq, c * LANE:(c + 1) * LANE] = p.astype(BF16)
        for kvh in range(2):
            rs = slice(kvh * 4 * tq, (kvh + 1) * 4 * tq)
            acc_ref[rs, :] = acc_ref[rs, :] + _dot(p_ref[rs, :], vb_ref[kvh, pl.ds(j * tk, tk), :])

    n_full = (qi * tq) // tk
    lax.fori_loop(0, n_full, lambda j, c: (kv_step(j, False), c)[1], 0)
    kv_step(n_full, True)

    def o_of(r):
        a = acc_ref[r * tq:(r + 1) * tq, :]
        return a / pltpu.roll(a, DIFF_DV, 1)

    outs = _diff_finish(o_of, lam_ref[0], gsub_ref[...], ones64_ref[...], lam_init)
    o_ref[0, :, 0:LANE] = outs[0]
    o_ref[0, :, LANE:2 * LANE] = outs[1]


def _diff_prompt(qd, kd, vd, lam, lw, lam_init, b, l):
    tq = min(128, l)
    tk = min(256, l)
    assert tk % tq == 0 and l % tk == 0
    kern = functools.partial(_diffp_kernel, tq, tk, lam_init)
    return pl.pallas_call(
        kern,
        grid=(b, l // tq),
        in_specs=[pl.BlockSpec(memory_space=pltpu.SMEM),
                  pl.BlockSpec((1, tq, 256), lambda bi, qi: (bi, qi, 0)),
                  pl.BlockSpec((1, l, 128), lambda bi, qi: (bi, 0, 0)),
                  pl.BlockSpec((1, l, 128), lambda bi, qi: (bi, 0, 0)),
                  pl.BlockSpec((1, 128), lambda bi, qi: (0, 0)),
                  pl.BlockSpec((128, 128), lambda bi, qi: (0, 0))],
        out_specs=pl.BlockSpec((1, tq, 256), lambda bi, qi: (bi, qi, 0)),
        out_shape=jax.ShapeDtypeStruct((b, l, 256), F32),
        scratch_shapes=[pltpu.VMEM((8 * tq, LANE), BF16), pltpu.VMEM((l, LANE), BF16),
                        pltpu.VMEM((2, l, LANE), BF16), pltpu.VMEM((8 * tq, tk), BF16),
                        pltpu.VMEM((8 * tq, LANE), F32), pltpu.VMEM((8 * tq, LANE), F32)],
        compiler_params=pltpu.CompilerParams(dimension_semantics=("arbitrary", "arbitrary"),
                                             vmem_limit_bytes=VMEM_LIMIT),
        name="diff_prompt",
    )(lam, qd.reshape(b, l, 256), kd.reshape(b, l, 128), vd.reshape(b, l, 128), lw['gsub'], lw['ones64_128'])


def _fill_mla_q(qbig_ref, q, rows):
    lane = lax.broadcasted_iota(jnp.int32, (1, 2 * LANE), 1)
    for h in range(MLA_HEADS):
        qbig_ref[h * rows:(h + 1) * rows, :] = jnp.where(
            (lane >= h * 64) & (lane < (h + 1) * 64), q, 0.0).astype(BF16)


def _mla_finish(acc_ref, l_ref, wuv_ref, rows):
    o = jnp.zeros((rows, 2 * LANE), F32)
    for h in range(MLA_HEADS):
        den = jnp.sum(l_ref[h * rows:(h + 1) * rows, :], axis=-1, keepdims=True)
        olat = acc_ref[h * rows:(h + 1) * rows, :] / den
        o = o + _dot(olat.astype(BF16), wuv_ref[h])
    return o


def _mlap_kernel(tq, tk, q_ref, k_ref, c_ref, wuv_ref, o_ref, qbig_ref, kb_ref, cb_ref, p_ref, m_ref, l_ref,
                 acc_ref):
    qi = pl.program_id(1)

    @pl.when(qi == 0)
    def _():
        kb_ref[...] = k_ref[0].astype(BF16)
        cb_ref[...] = c_ref[0].astype(BF16)

    _fill_mla_q(qbig_ref, q_ref[0], tq)
    m_ref[...] = jnp.full(m_ref.shape, NEG, F32)
    l_ref[...] = jnp.zeros(l_ref.shape, F32)
    acc_ref[...] = jnp.zeros(acc_ref.shape, F32)
    c1 = MLA_DQK ** -0.5 * LOG2E
    qpos = qi * tq + lax.broadcasted_iota(jnp.int32, (tq, 1), 0)

    def kv_step(j, masked):
        s_all = _dot_nt(qbig_ref[...], kb_ref[pl.ds(j * tk, tk), :])
        kpos = j * tk + lax.broadcasted_iota(jnp.int32, (1, tk), 1)
        for h in range(MLA_HEADS):
            rs = slice(h * tq, (h + 1) * tq)
            t = s_all[rs, :] * c1
            if masked:
                t = jnp.where(kpos <= qpos, t, NEG)
            alpha, ps = _online_update(t, m_ref, h * tq, tq)
            acc_ref[rs, :] = alpha * acc_ref[rs, :]
            lsum = alpha * l_ref[rs, :]
            for c, p in enumerate(ps):
                lsum = lsum + p
                p_ref[rs, c * LANE:(c + 1) * LANE] = p.astype(BF16)
            l_ref[rs, :] = lsum
        acc_ref[...] = acc_ref[...] + _dot(p_ref[...], cb_ref[pl.ds(j * tk, tk), :])

    n_full = (qi * tq) // tk
    lax.fori_loop(0, n_full, lambda j, c: (kv_step(j, False), c)[1], 0)
    kv_step(n_full, True)
    o_ref[0] = _mla_finish(acc_ref, l_ref, wuv_ref, tq)


def _mla_prompt(qm, kmla, c, lw, b, l):
    tq = min(128, l)
    tk = min(256, l)
    assert tk % tq == 0 and l % tk == 0
    nr = MLA_HEADS * tq
    return pl.pallas_call(
        functools.partial(_mlap_kernel, tq, tk),
        grid=(b, l // tq),
        in_specs=[pl.BlockSpec((1, tq, 256), lambda bi, qi: (bi, qi, 0)),
                  pl.BlockSpec((1, l, 256), lambda bi, qi: (bi, 0, 0)),
                  pl.BlockSpec((1, l, 128), lambda bi, qi: (bi, 0, 0)),
                  pl.BlockSpec((MLA_HEADS, 128, 256), lambda bi, qi: (0, 0, 0))],
        out_specs=pl.BlockSpec((1, tq, 256), lambda bi, qi: (bi, qi, 0)),
        out_shape=jax.ShapeDtypeStruct((b, l, 256), F32),
        scratch_shapes=[pltpu.VMEM((nr, 2 * LANE), BF16), pltpu.VMEM((l, 2 * LANE), BF16),
                        pltpu.VMEM((l, LANE), BF16), pltpu.VMEM((nr, tk), BF16),
                        pltpu.VMEM((nr, LANE), F32), pltpu.VMEM((nr, LANE), F32), pltpu.VMEM((nr, LANE), F32)],
        compiler_params=pltpu.CompilerParams(dimension_semantics=("arbitrary", "arbitrary"),
                                             vmem_limit_bytes=VMEM_LIMIT),
        name="mla_prompt",
    )(qm.reshape(b, l, 256), kmla.reshape(b, l, 256), c.reshape(b, l, 128), lw['wuv'])


def _sample_attn_kernel(ppc, nch, n_pages, ns, past_len, lam_init, layer, pt_ref, lam_ref,
                        qd_ref, qabs_ref, qr_ref, kn_ref, vn_ref, cn_ref, krn_ref,
                        wukt_ref, ones8_ref, wuv_ref, gsub_ref, ones64_ref,
                        ckt_hbm, cvt_hbm, cc_hbm, ckrt_hbm, od_ref, om_ref,
                        kbuf, vbuf, cbuf, krbuf, sems,
                        qd_s, qw_s, qr_s, m1, l1, acc1, m2, l2, acc2, pad_s):
    i = pl.program_id(0)
    ch = i % nch
    slot = i % 2
    nd = 8 * ns
    nm = MLA_HEADS * ns
    c1d = DIFF_DH ** -0.5 * LOG2E
    c1m = MLA_DQK ** -0.5 * LOG2E

    def page_copies(step, slot_):
        base = (step // nch) * n_pages + (step % nch) * ppc
        copies = []
        for j in range(ppc):
            pg = pt_ref[base + j]
            for a, (src, dst) in enumerate(((ckt_hbm, kbuf), (cvt_hbm, vbuf), (cc_hbm, cbuf), (ckrt_hbm, krbuf))):
                copies.append(pltpu.make_async_copy(src.at[layer, pg], dst.at[slot_, j], sems.at[slot_, a, j]))
        return copies

    @pl.when(i == 0)
    def _():
        for cp in page_copies(0, 0):
            cp.start()

    @pl.when(i + 1 < pl.num_programs(0))
    def _():
        for cp in page_copies(i + 1, 1 - slot):
            cp.start()

    @pl.when(ch == 0)
    def _():
        _fill_diff_q(qd_s, qd_ref[...], ns)
        for h in range(MLA_HEADS):
            qw_s[h * ns:(h + 1) * ns, :] = qabs_ref[:, h * LANE:(h + 1) * LANE].astype(BF16)
            qr_s[h * ns:(h + 1) * ns, :] = qr_ref[:, h * LANE:(h + 1) * LANE].astype(BF16)
        qw_s[nm:nm + LANE, :] = wukt_ref[...]
        m1[...] = jnp.full(m1.shape, NEG, F32)
        l1[...] = jnp.zeros(l1.shape, F32)
        acc1[...] = jnp.zeros(acc1.shape, F32)
        m2[...] = jnp.full(m2.shape, NEG, F32)
        l2[...] = jnp.zeros(l2.shape, F32)
        acc2[...] = jnp.zeros(acc2.shape, F32)

    rowd = lax.broadcasted_iota(jnp.int32, (nd, 1), 0)
    slope2 = jnp.zeros((nd, 1), F32)
    for r, (kvh, g, mp) in enumerate(_diff_rows()):
        slope2 = jnp.where((rowd >= r * ns) & (rowd < (r + 1) * ns), _diff_slope(g, kvh) * LOG2E, slope2)
    qrow_d = rowd & (ns - 1)
    qrow_m = lax.broadcasted_iota(jnp.int32, (nm, 1), 0) & (ns - 1)

    def mla_scores(r, rope_part, krsq):
        num = r[0:nm, :] + rope_part
        kt2 = r[nm:nm + LANE, :]
        sq = kt2 * kt2
        rows = []
        for h in range(MLA_HEADS):
            ssq = jnp.sum(sq[h * MLA_NOPE:(h + 1) * MLA_NOPE, :], axis=0, keepdims=True) + krsq
            inv = lax.rsqrt(ssq * (1.0 / MLA_DQK) + EPS) * c1m
            rows.append(num[h * ns:(h + 1) * ns, :] * inv)
        return jnp.concatenate(rows, axis=0)

    def accumulate(t, m_ref, l_ref, acc_ref, pv):
        alpha, ps = _online_update(t, m_ref, 0, t.shape[0])
        lsum = alpha * l_ref[...]
        o = alpha * acc_ref[...]
        for j, p in enumerate(ps):
            lsum = lsum + p
            o = o + pv(j, p.astype(BF16))
        l_ref[...] = lsum
        acc_ref[...] = o

    for cp in page_copies(i, slot):
        cp.wait()

    sd, sm, cbs = [], [], []
    for j in range(ppc):
        sd.append(_dot(qd_s[...], kbuf[slot, j].astype(BF16)))
        cb = cbuf[slot, j].astype(BF16)
        cbs.append(cb)
        krt = krbuf[slot, j]
        rope_part = _dot(qr_s[:, 0:MLA_ROPE], krt.astype(BF16))
        sm.append(mla_scores(_dot_nt(qw_s[...], cb), rope_part, jnp.sum(krt * krt, axis=0, keepdims=True)))
    n = ppc * PAGE
    kposf = (ch * n + lax.broadcasted_iota(jnp.int32, (1, n), 1)).astype(F32)
    t_d = jnp.concatenate(sd, axis=1) * c1d + slope2 * kposf
    accumulate(t_d, m1, l1, acc1, lambda j, p: _dot_nt(p, vbuf[slot, j].astype(BF16)))
    accumulate(jnp.concatenate(sm, axis=1), m2, l2, acc2, lambda j, p: _dot(p, cbs[j]))

    @pl.when(ch == nch - 1)
    def _():
        jpos = lax.broadcasted_iota(jnp.int32, (1, PAGE), 1)
        pad_s[...] = jnp.zeros(pad_s.shape, F32)
        pad_s[0, 0:ns, :] = kn_ref[...]
        pad_s[1, 0:ns, :] = vn_ref[...]
        pad_s[2, 0:ns, :] = cn_ref[...]
        pad_s[3, 0:ns, :] = krn_ref[...]
        knb = pad_s[0].astype(BF16)
        vnb = pad_s[1].astype(BF16)
        cnb = pad_s[2].astype(BF16)
        krn = pad_s[3]
        t = _dot_nt(qd_s[...], knb) * c1d + slope2 * (past_len + jpos).astype(F32)
        accumulate(jnp.where(jpos <= qrow_d, t, NEG), m1, l1, acc1, lambda j, p: _dot(p, vnb))
        hi, lo = _split2(krn * krn)
        krsq = (_dot_nt(ones8_ref[...], hi) + _dot_nt(ones8_ref[...], lo))[0:1, :]
        t = mla_scores(_dot_nt(qw_s[...], cnb), _dot_nt(qr_s[...], krn.astype(BF16)), krsq)
        accumulate(jnp.where(jpos <= qrow_m, t, NEG), m2, l2, acc2, lambda j, p: _dot(p, cnb))

        def o_of(r):
            den = jnp.sum(l1[r * ns:(r + 1) * ns, :], axis=-1, keepdims=True)
            return acc1[r * ns:(r + 1) * ns, :] / den

        outs = _diff_finish(o_of, lam_ref[0], gsub_ref[...], ones64_ref[...], lam_init)
        od_ref[:, 0:LANE] = outs[0]
        od_ref[:, LANE:2 * LANE] = outs[1]
        om_ref[...] = _mla_finish(acc2, l2, wuv_ref, ns)


def _sample_attn(layer, lam, lw, lam_init, page_table, caches, qd, qabs, qr, kn, vn, cn, krn, nb, ns):
    n_pages = page_table.shape[1]
    ppc = math.gcd(n_pages, 16)
    assert ns & (ns - 1) == 0
    nch = n_pages // ppc
    past_len = n_pages * PAGE
    pt_flat = page_table.reshape(-1)
    seq = lambda w: pl.BlockSpec((ns, w), lambda i, pt: (i // nch, 0))
    const = lambda shape: pl.BlockSpec(shape, lambda i, pt: (0,) * len(shape))
    hbm = pl.BlockSpec(memory_space=pl.ANY)
    in_specs = [pl.BlockSpec(memory_space=pltpu.SMEM),
                seq(256), seq(512), seq(512), seq(128), seq(128), seq(128), seq(128),
                const((128, 128)), const((8, 128)), const((MLA_HEADS, 128, 256)),
                const((1, 128)), const((128, 128)), hbm, hbm, hbm, hbm]
    ins = [lam, qd, qabs, qr, kn, vn, cn, krn, lw['wukt'], lw['ones8'], lw['wuv'], lw['gsub'],
           lw['ones64_128'], *caches]
    nd, nm = 8 * ns, MLA_HEADS * ns
    kern = functools.partial(_sample_attn_kernel, ppc, nch, n_pages, ns, past_len, lam_init, layer)
    page_buf = lambda rows: pltpu.VMEM((2, ppc, rows, PAGE), F32)
    return pl.pallas_call(
        kern,
        grid_spec=pltpu.PrefetchScalarGridSpec(
            num_scalar_prefetch=1,
            grid=(nb * nch,),
            in_specs=in_specs,
            out_specs=[seq(256), seq(256)],
            scratch_shapes=[page_buf(PAGE), page_buf(PAGE), page_buf(PAGE), page_buf(MLA_ROPE),
                            pltpu.SemaphoreType.DMA((2, 4, ppc)),
                            pltpu.VMEM((nd, LANE), BF16), pltpu.VMEM((nm + LANE, LANE), BF16),
                            pltpu.VMEM((nm, LANE), BF16),
                            pltpu.VMEM((nd, LANE), F32), pltpu.VMEM((nd, LANE), F32), pltpu.VMEM((nd, LANE), F32),
                            pltpu.VMEM((nm, LANE), F32), pltpu.VMEM((nm, LANE), F32), pltpu.VMEM((nm, LANE), F32),
                            pltpu.VMEM((4, PAGE, LANE), F32)]),
        out_shape=[jax.ShapeDtypeStruct((nb * ns, 256), F32), jax.ShapeDtypeStruct((nb * ns, 256), F32)],
        compiler_params=pltpu.CompilerParams(dimension_semantics=("arbitrary",), vmem_limit_bytes=VMEM_LIMIT),
        name="sample_attn",
    )(pt_flat, *ins)


N_ST = 2 * SSM_W // LANE


def _s5_kernel(tb, u_ref, h0re_ref, h0im_ref, bbar_ref, are_ref, aim_ref, cmat_ref, d_ref, wglu_ref,
               o_ref, hre_ref, him_ref, ubt, utm, bus, hs, otm, obt, hst):
    tblk = pl.program_id(1)
    half = N_ST // 2
    nlt = BRANCH_W // LANE

    @pl.when(tblk == 0)
    def _():
        for j in range(half):
            hst[j] = h0re_ref[:, j * LANE:(j + 1) * LANE]
            hst[half + j] = h0im_ref[:, j * LANE:(j + 1) * LANE]

    rows = SUB * tb
    u = u_ref[...]
    for c in range(nlt):
        ubt[c] = u[:, :, c * LANE:(c + 1) * LANE].reshape(rows, LANE)

    def to_time_major(t, carry):
        r0 = pl.multiple_of(t * SUB, SUB)
        for c in range(nlt):
            utm[pl.ds(r0, SUB), c * LANE:(c + 1) * LANE] = ubt[c, pl.ds(t, SUB, stride=tb), :]
        return carry

    lax.fori_loop(0, tb, to_time_major, 0)
    bus[...] = _dot(utm[...].astype(BF16), bbar_ref[...])
    ar = [jnp.broadcast_to(are_ref[:, j * LANE:(j + 1) * LANE], (SUB, LANE)) for j in range(half)]
    ai = [jnp.broadcast_to(aim_ref[:, j * LANE:(j + 1) * LANE], (SUB, LANE)) for j in range(half)]

    def step(t, st):
        r0 = pl.multiple_of(t * SUB, SUB)
        new = [None] * N_ST
        for j in range(half):
            hr, hi = st[j], st[half + j]
            nr = ar[j] * hr - ai[j] * hi + bus[pl.ds(r0, SUB), j * LANE:(j + 1) * LANE]
            ni = ar[j] * hi + ai[j] * hr + bus[pl.ds(r0, SUB), (half + j) * LANE:(half + j + 1) * LANE]
            hs[pl.ds(r0, SUB), j * LANE:(j + 1) * LANE] = nr
            hs[pl.ds(r0, SUB), (half + j) * LANE:(half + j + 1) * LANE] = ni
            new[j], new[half + j] = nr, ni
        return tuple(new)

    st = lax.fori_loop(0, tb, step, tuple(hst[j] for j in range(N_ST)))
    for j in range(N_ST):
        hst[j] = st[j]

    y = d_ref[...] * utm[...] + _dot(hs[...].astype(BF16), cmat_ref[...])
    yg = _dot(y.astype(BF16), wglu_ref[...])
    otm[...] = yg[:, :BRANCH_W] * _sigmoid(yg[:, BRANCH_W:])

    def to_batch_major(t, carry):
        r0 = pl.multiple_of(t * SUB, SUB)
        for c in range(nlt):
            obt[c, pl.ds(t, SUB, stride=tb), :] = otm[pl.ds(r0, SUB), c * LANE:(c + 1) * LANE]
        return carry

    lax.fori_loop(0, tb, to_batch_major, 0)
    for c in range(nlt):
        o_ref[:, :, c * LANE:(c + 1) * LANE] = obt[c].reshape(SUB, tb, LANE)

    @pl.when(tblk == pl.num_programs(1) - 1)
    def _():
        for j in range(half):
            hre_ref[:, j * LANE:(j + 1) * LANE] = st[j]
            him_ref[:, j * LANE:(j + 1) * LANE] = st[half + j]


def _s5(su, h0re, h0im, lw, b, l):
    assert b % SUB == 0
    tb = min(64, l)
    rows = SUB * tb
    const = lambda shape: pl.BlockSpec(shape, lambda bi, ti: (0,) * len(shape))
    st_spec = pl.BlockSpec((SUB, SSM_W), lambda bi, ti: (bi, 0))
    return pl.pallas_call(
        functools.partial(_s5_kernel, tb),
        grid=(b // SUB, l // tb),
        in_specs=[pl.BlockSpec((SUB, tb, BRANCH_W), lambda bi, ti: (bi, ti, 0)), st_spec, st_spec,
                  const((BRANCH_W, 2 * SSM_W)), const((1, SSM_W)), const((1, SSM_W)),
                  const((2 * SSM_W, BRANCH_W)), const((1, BRANCH_W)), const((BRANCH_W, 2 * BRANCH_W))],
        out_specs=[pl.BlockSpec((SUB, tb, BRANCH_W), lambda bi, ti: (bi, ti, 0)), st_spec, st_spec],
        out_shape=[jax.ShapeDtypeStruct((b, l, BRANCH_W), F32), jax.ShapeDtypeStruct((b, SSM_W), F32),
                   jax.ShapeDtypeStruct((b, SSM_W), F32)],
        scratch_shapes=[pltpu.VMEM((BRANCH_W // LANE, rows, LANE), F32), pltpu.VMEM((rows, BRANCH_W), F32),
                        pltpu.VMEM((rows, 2 * SSM_W), F32), pltpu.VMEM((rows, 2 * SSM_W), F32),
                        pltpu.VMEM((rows, BRANCH_W), F32), pltpu.VMEM((BRANCH_W // LANE, rows, LANE), F32),
                        pltpu.VMEM((N_ST, SUB, LANE), F32)],
        compiler_params=pltpu.CompilerParams(dimension_semantics=("arbitrary", "arbitrary"),
                                             vmem_limit_bytes=VMEM_LIMIT),
        name="s5",
    )(su.reshape(b, l, BRANCH_W), h0re, h0im, lw['bbar'], lw['a_re'], lw['a_im'], lw['cmat'], lw['ssm_d'], lw['wglu'])


def _hgrn_kernel(chunk, nchunk, chain, q_ref, f_ref, v_ref, s0_ref, lb_ref, gain_ref, tri_ref, same_ref,
                 ones64_ref, o_ref, sfin_ref, kp, bcp, vp, st, oacc):
    rows = chunk * nchunk
    q = q_ref[0]
    zf = f_ref[0]
    v = v_ref[0]
    lb = lb_ref[...]
    logf = jnp.log(lb + (1.0 - lb) * _sigmoid(zf))
    kk = (1.0 - lb) * _sigmoid(-zf)
    bc = _dot3t(tri_ref[...], logf)
    bl = _dot3t(same_ref[...], logf)
    qh = q * jnp.exp(bc)
    kdec = (kk * jnp.exp(bl - bc)).astype(BF16)

    kp[0:HG_CHUNK, :] = jnp.zeros((HG_CHUNK, BRANCH_W), F32)
    bcp[0:HG_CHUNK, :] = jnp.zeros((HG_CHUNK, BRANCH_W), F32)
    vp[0:HG_CHUNK, :] = jnp.zeros((HG_CHUNK, BRANCH_W), F32)
    kp[HG_CHUNK:HG_CHUNK + rows, :] = kk
    bcp[HG_CHUNK:HG_CHUNK + rows, :] = bc
    vp[HG_CHUNK:HG_CHUNK + rows, :] = v
    rpos = lax.broadcasted_iota(jnp.int32, (rows, 1), 0) & (chunk - 1)
    o = _dot((q * kk).astype(BF16), ones64_ref[...]) * v
    for d in range(1, chunk):
        lo = HG_CHUNK - d
        ks = kp[lo:lo + rows, :]
        bcs = bcp[lo:lo + rows, :]
        vs = vp[lo:lo + rows, :]
        x = q * ks * jnp.exp(jnp.where(rpos >= d, bc - bcs, NEG))
        o = o + _dot(x.astype(BF16), ones64_ref[...]) * vs
    oacc[...] = o

    lane = lax.broadcasted_iota(jnp.int32, (1, BRANCH_W), 1)
    rowi = lax.broadcasted_iota(jnp.int32, (rows, 1), 0)
    head_lanes = [(lane >= h * HG_DK) & (lane < (h + 1) * HG_DK) for h in range(HG_HEADS)]
    kvs = []
    for i in range(nchunk):
        r0 = i * chunk
        vmask = jnp.where((rowi >= r0) & (rowi < r0 + chunk), v, 0.0).astype(BF16)
        kvt = _dot_tn(vmask, kdec)
        comp = jnp.zeros((HG_DV, BRANCH_W), F32)
        for h in range(HG_HEADS):
            comp = jnp.where(head_lanes[h], kvt[h * HG_DV:(h + 1) * HG_DV, :], comp)
        kvs.append(comp)
    if chain:
        @pl.when(pl.program_id(1) == 0)
        def _():
            st[...] = s0_ref[0]
        s_t = st[...]
    for i in range(nchunk):
        r0 = i * chunk
        if not chain:
            s_t = s0_ref[i]
        full = jnp.concatenate([jnp.where(head_lanes[h], s_t, 0.0) for h in range(HG_HEADS)],
                               axis=0).astype(BF16)
        oacc[r0:r0 + chunk, :] = oacc[r0:r0 + chunk, :] + _dot_nt(qh[r0:r0 + chunk, :].astype(BF16), full)
        s_t = s_t * jnp.exp(bl[r0:r0 + 1, :]) + kvs[i]
        if not chain:
            sfin_ref[i] = s_t
    if chain:
        st[...] = s_t
        sfin_ref[0] = s_t
    ot = oacc[...]
    msq = _dot2(ot * ot, ones64_ref[...]) * (1.0 / HG_DV)
    o_ref[0] = ot * lax.rsqrt(msq + EPS) * gain_ref[...]


def _dot3t(m, x):
    hi, mid, lo = _split3(x)
    return _dot(m, hi) + _dot(m, mid) + _dot(m, lo)


def _hgrn(hq, hf, hi, s0t, lw, b, l):
    chunk = math.gcd(l, HG_CHUNK)
    if l >= PAGE:
        chain, rows, nblk, outer = True, PAGE, l // PAGE, b
    else:
        assert l == chunk
        per = min(b, PAGE // l)
        chain, rows, nblk, outer = False, per * l, 1, b // per
    nchunk = rows // chunk
    sblk = 1 if chain else nchunk
    r = np.arange(rows)
    same = (r[:, None] // chunk) == (r[None, :] // chunk)
    tri = same & (r[None, :] <= r[:, None])
    blk = lambda: pl.BlockSpec((1, rows, BRANCH_W), lambda bi, ti: (bi * nblk + ti, 0, 0))
    const = lambda shape: pl.BlockSpec(shape, lambda bi, ti: (0,) * len(shape))
    st_spec = pl.BlockSpec((sblk, HG_DV, BRANCH_W), lambda bi, ti: (bi, 0, 0))
    shp = (outer * nblk, rows, BRANCH_W)
    o, sfin = pl.pallas_call(
        functools.partial(_hgrn_kernel, chunk, nchunk, chain),
        grid=(outer, nblk),
        in_specs=[blk(), blk(), blk(), st_spec, const((1, BRANCH_W)), const((1, BRANCH_W)),
                  const((rows, rows)), const((rows, rows)), const((BRANCH_W, BRANCH_W))],
        out_specs=[blk(), st_spec],
        out_shape=[jax.ShapeDtypeStruct(shp, F32), jax.ShapeDtypeStruct((b, HG_DV, BRANCH_W), F32)],
        scratch_shapes=[pltpu.VMEM((HG_CHUNK + rows, BRANCH_W), F32), pltpu.VMEM((HG_CHUNK + rows, BRANCH_W), F32),
                        pltpu.VMEM((HG_CHUNK + rows, BRANCH_W), F32), pltpu.VMEM((HG_DV, BRANCH_W), F32),
                        pltpu.VMEM((rows, BRANCH_W), F32)],
        compiler_params=pltpu.CompilerParams(dimension_semantics=("arbitrary", "arbitrary"),
                                             vmem_limit_bytes=VMEM_LIMIT),
        name="hgrn_chain" if chain else "hgrn_step",
    )(hq.reshape(shp), hf.reshape(shp), hi.reshape(shp), s0t, lw['lb'], lw['hg_gain'],
      jnp.asarray(tri, BF16), jnp.asarray(same, BF16), lw['ones64'])
    return o.reshape(b * l, BRANCH_W), sfin


def _block_ones(n, blk):
    r = np.arange(n)
    return jnp.asarray((r[:, None] // blk) == (r[None, :] // blk), BF16)


def _layout_indices():
    zero = IN_COLS
    a = np.full((NA,), zero, np.int64)
    for g in range(2):
        for kvh in range(2):
            for mp in range(2):
                dst = _A['dq'] + g * 128 + kvh * 64 + mp * 32
                src = _OFF['dq'] + kvh * 128 + g * 64 + mp * 32
                a[dst:dst + 32] = np.arange(src, src + 32)
    a[_A['dk']:_A['dk'] + 128] = np.arange(_OFF['dk'], _OFF['dk'] + 128)
    a[_A['dv']:_A['dv'] + 128] = np.arange(_OFF['dv'], _OFF['dv'] + 128)
    a[_A['mqa']:_A['mqa'] + MLA_QL] = np.arange(_OFF['mqa'], _OFF['mqa'] + MLA_QL)
    a[_A['mkva']:_A['mkva'] + 128] = np.arange(_OFF['mkva'], _OFF['mkva'] + 128)
    for h in range(MLA_HEADS):
        dst = _A['krt'] + h * 64 + MLA_NOPE
        a[dst:dst + MLA_ROPE] = np.arange(_OFF['mkr'], _OFF['mkr'] + MLA_ROPE)
    a[_A['kr0']:_A['kr0'] + MLA_ROPE] = np.arange(_OFF['mkr'], _OFF['mkr'] + MLA_ROPE)
    for name in ('su', 'hq', 'hf', 'hi'):
        a[_A[name]:_A[name] + 256] = np.arange(_OFF[name], _OFF[name] + 256)
    perm = np.zeros((256,), np.int64)
    for g in range(2):
        for kvh in range(2):
            perm[g * 128 + kvh * 64:g * 128 + kvh * 64 + 64] = np.arange(kvh * 128 + g * 64, kvh * 128 + g * 64 + 64)
    bcols = np.concatenate([_OFF['gates'] + perm, np.arange(_OFF['gates'] + 256, _OFF['gates'] + 1024),
                            np.arange(_OFF['merge'], _OFF['merge'] + N_BRANCH * D_MODEL)])
    return a, bcols, perm


_IDX_A, _IDX_B, _PERM_DIFF = _layout_indices()


def _head48(vec):
    return jnp.tile(jnp.pad(vec, (0, 64 - MLA_DQK)), MLA_HEADS).reshape(1, 256)


def _prep_layer(p, l, lb_all):
    w_in = jnp.pad(p['w_in'][l], ((0, 0), (0, 1)))
    lw = {}
    lw['g'] = p['norm_gain'][l].reshape(1, D_MODEL)
    lw['wa'] = jnp.take(w_in, _IDX_A, axis=1).astype(BF16)
    lw['wb'] = jnp.take(w_in, _IDX_B, axis=1).astype(BF16)
    wbr = p['w_branch'][l]
    lw['wbr'] = jnp.concatenate([jnp.take(wbr[0], _PERM_DIFF, axis=0)[None], wbr[1:]], axis=0).astype(BF16)
    lw['wout'] = p['w_out'][l].astype(BF16)
    lw['gq'] = jnp.tile(p['diff_q_gain'][l], 8).reshape(1, 256)
    lw['gk'] = jnp.tile(p['diff_k_gain'][l], 4).reshape(1, 128)
    lw['gsub'] = jnp.tile(p['diff_subln_gain'][l], 2).reshape(1, 128)
    lw['gqa'] = jnp.pad(p['mla_qa_gain'][l], (0, 256 - MLA_QL)).reshape(1, 256)
    wuq = p['w_mla_uq'][l].reshape(MLA_QL, MLA_HEADS, MLA_DQK)
    lw['wuq'] = jnp.pad(wuq, ((0, 256 - MLA_QL), (0, 0), (0, 64 - MLA_DQK))).reshape(256, 256).astype(BF16)
    lw['gq48'] = _head48(p['mla_q_gain'][l])
    lw['gk48'] = _head48(p['mla_k_gain'][l])
    lw['gkva'] = p['mla_kva_gain'][l].reshape(1, 128)
    wuk = p['w_mla_uk'][l]
    lw['wukp'] = jnp.pad(wuk, ((0, 0), (0, 0), (0, 64 - MLA_NOPE))).reshape(128, 256).astype(BF16)
    lw['wuk2d'] = wuk.reshape(128, MLA_HEADS * MLA_NOPE).astype(BF16)
    wabs = jnp.zeros((MLA_HEADS, 64, MLA_HEADS, 128), F32)
    selr = np.zeros((MLA_HEADS, 64, MLA_HEADS, 128), np.float32)
    for h in range(MLA_HEADS):
        wabs = wabs.at[h, :MLA_NOPE, h, :].set(wuk[:, h, :].T)
        selr[h, MLA_NOPE + np.arange(MLA_ROPE), h, np.arange(MLA_ROPE)] = 1.0
    lw['wabs'] = wabs.reshape(256, 512).astype(BF16)
    lw['selr'] = jnp.asarray(selr.reshape(256, 512), BF16)
    wuv = p['w_mla_uv'][l]
    wuvp = jnp.zeros((MLA_HEADS, 128, MLA_HEADS, MLA_DV), F32)
    for h in range(MLA_HEADS):
        wuvp = wuvp.at[h, :, h, :].set(wuv[:, h, :])
    lw['wuv'] = wuvp.reshape(MLA_HEADS, 128, 256).astype(BF16)
    lw['wukt'] = wuk.reshape(128, MLA_HEADS * MLA_NOPE).T.astype(BF16)
    ones8 = np.zeros((8, 128), np.float32)
    ones8[:, :MLA_ROPE] = 1.0
    lw['ones8'] = jnp.asarray(ones8, BF16)
    lw['ones32'] = _block_ones(256, 32)
    lw['ones64'] = _block_ones(256, 64)
    lw['ones64_128'] = _block_ones(128, 64)

    lre, lim = p['ssm_a_re'][l], p['ssm_a_im'][l]
    dt = jnp.exp(p['ssm_log_dt'][l])[:, None]
    mag = jnp.exp(lre * dt)
    are, aim = mag * jnp.cos(lim * dt), mag * jnp.sin(lim * dt)
    den = lre * lre + lim * lim
    cre = ((are - 1.0) * lre + aim * lim) / den
    cim = (aim * lre - (are - 1.0) * lim) / den
    bre, bim = p['ssm_b_re'][l], p['ssm_b_im'][l]
    bbre = cre[..., None] * bre - cim[..., None] * bim
    bbim = cre[..., None] * bim + cim[..., None] * bre
    eye = jnp.eye(SSM_GROUPS, dtype=F32)
    b_re = jnp.einsum('gph,gk->ghkp', bbre, eye).reshape(BRANCH_W, SSM_W)
    b_im = jnp.einsum('gph,gk->ghkp', bbim, eye).reshape(BRANCH_W, SSM_W)
    lw['bbar'] = jnp.concatenate([b_re, b_im], axis=1).astype(BF16)
    c_re = jnp.einsum('ghp,gk->gpkh', p['ssm_c_re'][l], eye).reshape(SSM_W, BRANCH_W)
    c_im = jnp.einsum('ghp,gk->gpkh', p['ssm_c_im'][l], eye).reshape(SSM_W, BRANCH_W)
    lw['cmat'] = jnp.concatenate([c_re, -c_im], axis=0).astype(BF16)
    lw['a_re'] = are.reshape(1, SSM_W)
    lw['a_im'] = aim.reshape(1, SSM_W)
    lw['ssm_d'] = p['ssm_d'][l].reshape(1, BRANCH_W)
    lw['wglu'] = p['w_glu'][l].astype(BF16)
    lw['lb'] = lb_all[l].reshape(1, BRANCH_W)
    lw['hg_gain'] = jnp.tile(p['hg_norm_gain'][l], HG_HEADS).reshape(1, BRANCH_W)
    lp = p['diff_lambda'][l]
    lam_init = 0.8 - 0.6 * math.exp(-0.3 * l)
    lw['lam'] = (jnp.exp(jnp.sum(lp[0] * lp[1])) - jnp.exp(jnp.sum(lp[2] * lp[3])) + lam_init).reshape(1)
    return lw, lam_init


def _rope_tables(pos):
    half = MLA_ROPE // 2
    freqs = ROPE_BASE ** (-jnp.arange(half, dtype=F32) / half)
    ang = pos.astype(F32)[:, None] * freqs
    cos, sin = jnp.cos(ang), jnp.sin(ang)
    n = pos.shape[0]
    one = jnp.ones((n, MLA_NOPE), F32)
    zero = jnp.zeros((n, MLA_NOPE), F32)
    pad1 = jnp.ones((n, 64 - MLA_DQK), F32)
    pad0 = jnp.zeros((n, 64 - MLA_DQK), F32)
    cs_h = jnp.concatenate([one, cos, cos, pad1], axis=1)
    sn_h = jnp.concatenate([zero, -sin, sin, pad0], axis=1)
    cs0 = jnp.concatenate([cos, cos, jnp.ones((n, LANE - MLA_ROPE), F32)], axis=1)
    sn0 = jnp.concatenate([-sin, sin, jnp.zeros((n, LANE - MLA_ROPE), F32)], axis=1)
    return dict(cs=jnp.tile(cs_h, (1, 2)), sn=jnp.tile(sn_h, (1, 2)), cs0=cs0, sn0=sn0)


def kernel(x_prompt, x_sample, cache_diff_k, cache_diff_v, cache_mla_c, cache_mla_kr, state_ssm_re, state_ssm_im, state_hgrn, page_table, norm_gain, w_in, w_branch, w_out, diff_q_gain, diff_k_gain, diff_lambda, diff_subln_gain, mla_qa_gain, mla_kva_gain, w_mla_uq, w_mla_uk, w_mla_uv, mla_q_gain, mla_k_gain, ssm_a_re, ssm_a_im, ssm_log_dt, ssm_b_re, ssm_b_im, ssm_c_re, ssm_c_im, ssm_d, w_glu, hg_lb_logits, hg_norm_gain):
    p = dict(norm_gain=norm_gain, w_in=w_in, w_branch=w_branch, w_out=w_out, diff_q_gain=diff_q_gain,
             diff_k_gain=diff_k_gain, diff_lambda=diff_lambda, diff_subln_gain=diff_subln_gain,
             mla_qa_gain=mla_qa_gain, mla_kva_gain=mla_kva_gain, w_mla_uq=w_mla_uq, w_mla_uk=w_mla_uk,
             w_mla_uv=w_mla_uv, mla_q_gain=mla_q_gain, mla_k_gain=mla_k_gain, ssm_a_re=ssm_a_re, ssm_a_im=ssm_a_im,
             ssm_log_dt=ssm_log_dt, ssm_b_re=ssm_b_re, ssm_b_im=ssm_b_im, ssm_c_re=ssm_c_re, ssm_c_im=ssm_c_im,
             ssm_d=ssm_d, w_glu=w_glu, hg_norm_gain=hg_norm_gain)
    depth = w_in.shape[0]
    bp, lp, _ = x_prompt.shape
    bs, ls, _ = x_sample.shape
    n_pages = page_table.shape[1]
    past_len = n_pages * PAGE
    tp, ts = bp * lp, bs * ls

    sm = jax.nn.softmax(hg_lb_logits.astype(F32), axis=0)
    lb_all = jnp.cumsum(sm, axis=0) - sm[0]

    tm_p = min(512, lp)
    tm_s = min(512, ts)
    assert tm_s % ls == 0
    tabs_p = _rope_tables(jnp.arange(lp))
    tabs_s = _rope_tables(past_len + (jnp.arange(tm_s) % ls))

    n_pool = cache_diff_k.shape[1]
    ckt = jnp.transpose(cache_diff_k, (0, 1, 3, 4, 2)).reshape(depth, n_pool, 128, PAGE)
    cvt = jnp.transpose(cache_diff_v, (0, 1, 3, 4, 2)).reshape(depth, n_pool, 128, PAGE)
    ckrt = jnp.transpose(cache_mla_kr, (0, 1, 3, 2))
    caches = (ckt, cvt, cache_mla_c, ckrt)

    hp = x_prompt.reshape(tp, D_MODEL)
    hs = x_sample.reshape(ts, D_MODEL)
    zeros_p = jnp.zeros((bp, SSM_W), F32)
    zeros_st = jnp.zeros((bp, HG_DV, BRANCH_W), F32)
    outs_p, outs_s = [], []
    for l in range(depth):
        lw, lam_init = _prep_layer(p, l, lb_all)
        (qd, kd, vd, qm, c, kmla, kr16, su, hq, hf, hi) = _inproj(hp, lw, tabs_p, tm_p, lp // tm_p, False)
        o_diff = _diff_prompt(qd, kd, vd, lw['lam'], lw, lam_init, bp, lp).reshape(tp, 256)
        o_mla = _mla_prompt(qm, kmla, c, lw, bp, lp).reshape(tp, 256)
        o_ssm, hre, him = _s5(su, zeros_p, zeros_p, lw, bp, lp)
        o_hg, sfin = _hgrn(hq, hf, hi, zeros_st, lw, bp, lp)
        hp = _merge(hp, lw, (o_diff, o_mla, o_ssm.reshape(tp, 256), o_hg), tm_p)
        outs_p.append((kd, vd, c, kr16, hre, him, sfin))
        (qd, kd, vd, c, kr16, kr128, qabs, qr, su, hq, hf, hi) = _inproj(hs, lw, tabs_s, tm_s, 1, True)
        o_diff, o_mla = _sample_attn(l, lw['lam'], lw, lam_init, page_table, caches,
                                     qd, qabs, qr, kd, vd, c, kr128, bs, ls)
        o_ssm, hre, him = _s5(su, state_ssm_re[l].reshape(bs, SSM_W), state_ssm_im[l].reshape(bs, SSM_W), lw, bs, ls)
        s0t = jnp.transpose(state_hgrn[l], (0, 3, 1, 2)).reshape(bs, HG_DV, BRANCH_W)
        o_hg, sfin = _hgrn(hq, hf, hi, s0t, lw, bs, ls)
        hs = _merge(hs, lw, (o_diff, o_mla, o_ssm.reshape(ts, 256), o_hg), tm_s)
        outs_s.append((kd, vd, c, kr16, hre, him, sfin))

    def stack(group, i):
        return jnp.stack([st[i] for st in group], axis=0)

    def unstate(s, b):
        return jnp.transpose(s.reshape(depth, b, HG_DV, HG_HEADS, HG_DK), (0, 1, 3, 4, 2))

    npg = lp // PAGE
    return (hp.reshape(bp, lp, D_MODEL), hs.reshape(bs, ls, D_MODEL),
            stack(outs_p, 0).reshape(depth, bp, npg, PAGE, DIFF_KVH, 2 * DIFF_DH),
            stack(outs_p, 1).reshape(depth, bp, npg, PAGE, DIFF_KVH, DIFF_DV),
            stack(outs_p, 2).reshape(depth, bp, npg, PAGE, MLA_KVL),
            stack(outs_p, 3).reshape(depth, bp, npg, PAGE, MLA_ROPE),
            stack(outs_p, 4).reshape(depth, bp, SSM_GROUPS, SSM_STATE),
            stack(outs_p, 5).reshape(depth, bp, SSM_GROUPS, SSM_STATE),
            unstate(stack(outs_p, 6), bp),
            stack(outs_s, 0).reshape(depth, bs, ls, DIFF_KVH, 2 * DIFF_DH),
            stack(outs_s, 1).reshape(depth, bs, ls, DIFF_KVH, DIFF_DV),
            stack(outs_s, 2).reshape(depth, bs, ls, MLA_KVL),
            stack(outs_s, 3).reshape(depth, bs, ls, MLA_ROPE),
            stack(outs_s, 4).reshape(depth, bs, SSM_GROUPS, SSM_STATE),
            stack(outs_s, 5).reshape(depth, bs, SSM_GROUPS, SSM_STATE),
            unstate(stack(outs_s, 6), bs))
```

```python
import functools
import math

import numpy as np
import jax
import jax.numpy as jnp
from jax import lax
from jax.experimental import pallas as pl
from jax.experimental.pallas import tpu as pltpu

F32 = jnp.float32
BF16 = jnp.bfloat16
EPS = 1e-6
NEG = -1e30

D_MODEL = 1024
BRANCH_W = 256
N_BRANCH = 4
DIFF_HEADS, DIFF_KVH, DIFF_DH, DIFF_DV = 4, 2, 32, 64
MLA_HEADS, MLA_QL, MLA_KVL, MLA_NOPE, MLA_ROPE, MLA_DV = 4, 192, 128, 32, 16, 64
MLA_DQK = MLA_NOPE + MLA_ROPE
ROPE_BASE = 10000.0
SSM_GROUP, SSM_GROUPS, SSM_STATE = 16, 16, 64
SSM_W = SSM_GROUPS * SSM_STATE
HG_HEADS, HG_DK, HG_DV, HG_CHUNK = 4, 64, 64, 16
PAGE = 128
LANE = 128
SUB = 8
VMEM_LIMIT = 56 * 1024 * 1024

_OFF = dict(dq=0, dk=256, dv=384, mqa=512, mkva=704, mkr=832, su=848, hq=1104, hf=1360, hi=1616,
            gates=1872, merge=2896)
IN_COLS = 6992
_A = dict(dq=0, dk=256, dv=384, mqa=512, mkva=768, krt=896, kr0=1152, su=1280, hq=1536, hf=1792, hi=2048)
NA = 2304


def _dot(a, b):
    return jnp.dot(a, b, preferred_element_type=F32)


def _dot_nt(a, b):
    return lax.dot_general(a, b, (((1,), (1,)), ((), ())), preferred_element_type=F32)


def _dot_tn(a, b):
    return lax.dot_general(a, b, (((0,), (0,)), ((), ())), preferred_element_type=F32)


def _split2(x):
    hi = x.astype(BF16)
    lo = (x - hi.astype(F32)).astype(BF16)
    return hi, lo


def _split3(x):
    hi = x.astype(BF16)
    r = x - hi.astype(F32)
    mid = r.astype(BF16)
    lo = (r - mid.astype(F32)).astype(BF16)
    return hi, mid, lo


def _dot2(x, w):
    hi, lo = _split2(x)
    return _dot(hi, w) + _dot(lo, w)


def _dot3(x, w):
    hi, mid, lo = _split3(x)
    return _dot(hi, w) + _dot(mid, w) + _dot(lo, w)


def _sigmoid(x):
    return 1.0 / (1.0 + jnp.exp(-x))


def _rope_lanes(x, cs, sn, x1mask):
    rot = jnp.where(x1mask, pltpu.roll(x, LANE - MLA_ROPE // 2, 1), pltpu.roll(x, MLA_ROPE // 2, 1))
    return x * cs + rot * sn


def _inproj_kernel(sample, *refs):
    (x_ref, g_ref, wa_ref, gq_ref, gk_ref, gqa_ref, wuq_ref, gq48_ref, gkva_ref, wukp_ref, gk48_ref,
     cs_ref, sn_ref, cs0_ref, sn0_ref, ones32_ref, ones64_ref) = refs[:17]
    if sample:
        wabs_ref, selr_ref = refs[17:19]
        (qd_ref, kd_ref, vd_ref, c_ref, kr16_ref, kr128_ref, qabs_ref, qr_ref,
         su_ref, hq_ref, hf_ref, hi_ref) = refs[19:]
    else:
        (qd_ref, kd_ref, vd_ref, qm_ref, c_ref, kmla_ref, kr16_ref,
         su_ref, hq_ref, hf_ref, hi_ref) = refs[17:]

    x = x_ref[...]
    ms = jnp.mean(x * x, axis=-1, keepdims=True)
    xn = (x * lax.rsqrt(ms + EPS) * g_ref[...]).astype(BF16)
    h = _dot(xn, wa_ref[...])

    def seg(name, width):
        return h[:, _A[name]:_A[name] + width]

    dq = seg('dq', 256)
    msq = _dot2(dq * dq, ones32_ref[...]) * (1.0 / DIFF_DH)
    qd_ref[...] = dq * lax.rsqrt(msq + EPS) * gq_ref[...]
    dk = seg('dk', 128)
    msk = _dot2(dk * dk, ones32_ref[0:128, 0:128]) * (1.0 / DIFF_DH)
    kd_ref[...] = dk * lax.rsqrt(msk + EPS) * gk_ref[...]
    vd_ref[...] = seg('dv', 128)

    lane = lax.broadcasted_iota(jnp.int32, (1, LANE), 1)
    x1_head = ((lane & 63) >= MLA_NOPE) & ((lane & 63) < MLA_NOPE + MLA_ROPE // 2)
    x1_zero = lane < MLA_ROPE // 2
    cs = cs_ref[...]
    sn = sn_ref[...]

    mqa = seg('mqa', 256)
    msa = jnp.sum(mqa * mqa, axis=-1, keepdims=True) * (1.0 / MLA_QL)
    qa = (mqa * lax.rsqrt(msa + EPS) * gqa_ref[...]).astype(BF16)
    q = _dot(qa, wuq_ref[...])
    q = jnp.concatenate([_rope_lanes(q[:, :LANE], cs, sn, x1_head),
                         _rope_lanes(q[:, LANE:], cs, sn, x1_head)], axis=1)
    msq2 = _dot2(q * q, ones64_ref[...]) * (1.0 / MLA_DQK)
    qm = q * lax.rsqrt(msq2 + EPS) * gq48_ref[...]

    mkva = seg('mkva', 128)
    msc = jnp.mean(mkva * mkva, axis=-1, keepdims=True)
    c = mkva * lax.rsqrt(msc + EPS) * gkva_ref[...]
    c_ref[...] = c

    kr0 = _rope_lanes(seg('kr0', 128), cs0_ref[...], sn0_ref[...], x1_zero)
    kr16_ref[...] = kr0[:, :MLA_ROPE]

    if sample:
        kr128_ref[...] = kr0
        qg = qm * gk48_ref[...]
        qabs_ref[...] = _dot2(qg, wabs_ref[...])
        qr_ref[...] = _dot3(qg, selr_ref[...])
    else:
        qm_ref[...] = qm
        krt = seg('krt', 256)
        krt = jnp.concatenate([_rope_lanes(krt[:, :LANE], cs, sn, x1_head),
                               _rope_lanes(krt[:, LANE:], cs, sn, x1_head)], axis=1)
        kpre = _dot(c.astype(BF16), wukp_ref[...]) + krt
        msk2 = _dot2(kpre * kpre, ones64_ref[...]) * (1.0 / MLA_DQK)
        kmla_ref[...] = kpre * lax.rsqrt(msk2 + EPS) * gk48_ref[...]

    su_ref[...] = seg('su', 256)
    hq_ref[...] = seg('hq', 256)
    hf_ref[...] = seg('hf', 256)
    hi_ref[...] = seg('hi', 256)


def _inproj(x, lw, tabs, tm, n_tab, sample):
    t = x.shape[0]
    assert t % tm == 0
    const = lambda shape: pl.BlockSpec(shape, lambda i: (0,) * len(shape))
    row = lambda w: pl.BlockSpec((tm, w), lambda i: (i, 0))
    tab = pl.BlockSpec((tm, LANE), lambda i: (i % n_tab, 0))
    ins = [x, lw['g'], lw['wa'], lw['gq'], lw['gk'], lw['gqa'], lw['wuq'], lw['gq48'], lw['gkva'], lw['wukp'],
           lw['gk48'], tabs['cs'], tabs['sn'], tabs['cs0'], tabs['sn0'], lw['ones32'], lw['ones64']]
    in_specs = [row(D_MODEL), const((1, D_MODEL)), const((D_MODEL, NA)), const((1, 256)), const((1, 128)),
                const((1, 256)), const((256, 256)), const((1, 256)), const((1, 128)), const((128, 256)),
                const((1, 256)), tab, tab, tab, tab, const((256, 256)), const((256, 256))]
    if sample:
        ins += [lw['wabs'], lw['selr']]
        in_specs += [const((256, 512)), const((256, 512))]
        widths = [256, 128, 128, 128, MLA_ROPE, 128, 512, 512, 256, 256, 256, 256]
    else:
        widths = [256, 128, 128, 256, 128, 256, MLA_ROPE, 256, 256, 256, 256]
    return pl.pallas_call(
        functools.partial(_inproj_kernel, sample),
        grid=(t // tm,),
        in_specs=in_specs,
        out_specs=[row(w) for w in widths],
        out_shape=[jax.ShapeDtypeStruct((t, w), F32) for w in widths],
        compiler_params=pltpu.CompilerParams(dimension_semantics=("arbitrary",), vmem_limit_bytes=VMEM_LIMIT),
        name="inproj_sample" if sample else "inproj_prompt",
    )(*ins)


def _merge_kernel(x_ref, g_ref, wb_ref, b0_ref, b1_ref, b2_ref, b3_ref, wbr_ref, wout_ref, y_ref):
    x = x_ref[...]
    ms = jnp.mean(x * x, axis=-1, keepdims=True)
    xn = (x * lax.rsqrt(ms + EPS) * g_ref[...]).astype(BF16)
    m = jnp.zeros(x.shape, F32)
    for k, b_ref in enumerate((b0_ref, b1_ref, b2_ref, b3_ref)):
        gates = _dot(xn, wb_ref[:, k * BRANCH_W:(k + 1) * BRANCH_W])
        br = b_ref[...] * (gates * _sigmoid(gates))
        up = _dot(br.astype(BF16), wbr_ref[k])
        mg = _dot(xn, wb_ref[:, N_BRANCH * BRANCH_W + k * D_MODEL:N_BRANCH * BRANCH_W + (k + 1) * D_MODEL])
        m = m + _sigmoid(mg) * up
    y_ref[...] = x + _dot(m.astype(BF16), wout_ref[...])


def _merge(x, lw, branches, tm):
    t = x.shape[0]
    const = lambda shape: pl.BlockSpec(shape, lambda i: (0,) * len(shape))
    row = lambda w: pl.BlockSpec((tm, w), lambda i: (i, 0))
    nb = N_BRANCH * BRANCH_W + N_BRANCH * D_MODEL
    return pl.pallas_call(
        _merge_kernel,
        grid=(t // tm,),
        in_specs=[row(D_MODEL), const((1, D_MODEL)), const((D_MODEL, nb)), row(256), row(256), row(256), row(256),
                  const((N_BRANCH, BRANCH_W, D_MODEL)), const((D_MODEL, D_MODEL))],
        out_specs=row(D_MODEL),
        out_shape=jax.ShapeDtypeStruct((t, D_MODEL), F32),
        compiler_params=pltpu.CompilerParams(dimension_semantics=("arbitrary",), vmem_limit_bytes=VMEM_LIMIT),
        name="merge",
    )(x, lw['g'], lw['wb'], *branches, lw['wbr'], lw['wout'])


def _diff_slope(g, kvh):
    head = kvh * (DIFF_HEADS // DIFF_KVH) + g
    return 2.0 ** (-8.0 * (head + 1) / DIFF_HEADS)


LOG2E = 1.4426950408889634


def _diff_rows():
    return [(kvh, g, mp) for kvh in range(2) for g in range(2) for mp in range(2)]


def _fill_diff_q(qbig_ref, q, rows):
    lane = lax.broadcasted_iota(jnp.int32, (1, LANE), 1)
    for r, (kvh, g, mp) in enumerate(_diff_rows()):
        lo = kvh * 64 + mp * 32
        qbig_ref[r * rows:(r + 1) * rows, :] = jnp.where(
            (lane >= lo) & (lane < lo + DIFF_DH), q[:, g * LANE:(g + 1) * LANE], 0.0).astype(BF16)


def _diff_finish(o_of, lam, gsub, ones64, lam_init):
    lane = lax.broadcasted_iota(jnp.int32, (1, LANE), 1)
    outs = []
    for g in range(2):
        per_kvh = [o_of(kvh * 4 + g * 2) - lam * o_of(kvh * 4 + g * 2 + 1) for kvh in range(2)]
        og = jnp.where(lane < DIFF_DV, per_kvh[0], per_kvh[1])
        msq = _dot2(og * og, ones64) * (1.0 / DIFF_DV)
        outs.append(og * lax.rsqrt(msq + EPS) * gsub * (1.0 - lam_init))
    return outs


def _online_update(t, m_ref, r0, rows):
    m_old = m_ref[r0:r0 + rows, :]
    m_new = jnp.maximum(m_old, jnp.max(t, axis=-1, keepdims=True))
    m_ref[r0:r0 + rows, :] = m_new
    alpha = jnp.exp2(m_old - m_new)
    ps = [jnp.exp2(t[:, c * LANE:(c + 1) * LANE] - m_new) for c in range(t.shape[1] // LANE)]
    return alpha, ps


def _diffp_kernel(tq, tk, lam_init, lam_ref, q_ref, k_ref, v_ref, gsub_ref, ones64_ref, o_ref,
                  qbig_ref, kb_ref, vb_ref, p_ref, m_ref, acc_ref):
    qi = pl.program_id(1)
    lane = lax.broadcasted_iota(jnp.int32, (1, LANE), 1)

    @pl.when(qi == 0)
    def _():
        kb_ref[...] = k_ref[0].astype(BF16)
        v = v_ref[0]
        vb_ref[0] = jnp.where(lane < DIFF_DV, v, 1.0).astype(BF16)
        vb_ref[1] = jnp.where(lane < DIFF_DV, 1.0, v).astype(BF16)

    _fill_diff_q(qbig_ref, q_ref[0], tq)
    m_ref[...] = jnp.full(m_ref.shape, NEG, F32)
    acc_ref[...] = jnp.zeros(acc_ref.shape, F32)
    c1 = DIFF_DH ** -0.5 * LOG2E
    qpos = qi * tq + lax.broadcasted_iota(jnp.int32, (tq, 1), 0)

    def kv_step(j, masked):
        s_all = _dot_nt(qbig_ref[...], kb_ref[pl.ds(j * tk, tk), :])
        kpos = j * tk + lax.broadcasted_iota(jnp.int32, (1, tk), 1)
        kposf = kpos.astype(F32)
        for r, (kvh, g, mp) in enumerate(_diff_rows()):
            t = s_all[r * tq:(r + 1) * tq, :] * c1 + (_diff_slope(g, kvh) * LOG2E) * kposf
            if masked:
                t = jnp.where(kpos <= qpos, t, NEG)
            alpha, ps = _online_update(t, m_ref, r * tq, tq)
            acc_ref[r * tq:(r + 1) * tq, :] = alpha * acc_ref[r * tq:(r + 1) * tq, :]
            for c, p in enumerate(ps):
                p_ref[r * tq:(r + 1) * tq, c * LANE:(c + 1) * LANE] = p.astype(BF16)
        for kvh in range(2):
            rs = slice(kvh * 4 * tq, (kvh + 1) * 4 * tq)
            acc_ref[rs, :] = acc_ref[rs, :] + _dot(p_ref[rs, :], vb_ref[kvh, pl.ds(j * tk, tk), :])

    n_full = (qi * tq) // tk
    lax.fori_loop(0, n_full, lambda j, c: (kv_step(j, False), c)[1], 0)
    kv_step(n_full, True)

    def o_of(r):
        a = acc_ref[r * tq:(r + 1) * tq, :]
        return a / pltpu.roll(a, DIFF_DV, 1)

    outs = _diff_finish(o_of, lam_ref[0], gsub_ref[...], ones64_ref[...], lam_init)
    o_ref[0, :, 0:LANE] = outs[0]
    o_ref[0, :, LANE:2 * LANE] = outs[1]


def _diff_prompt(qd, kd, vd, lam, lw, lam_init, b, l):
    tq = min(256, l)
    tk = min(256, l)
    assert tk % tq == 0 and l % tk == 0
    kern = functools.partial(_diffp_kernel, tq, tk, lam_init)
    return pl.pallas_call(
        kern,
        grid=(b, l // tq),
        in_specs=[pl.BlockSpec(memory_space=pltpu.SMEM),
                  pl.BlockSpec((1, tq, 256), lambda bi, qi: (bi, qi, 0)),
                  pl.BlockSpec((1, l, 128), lambda bi, qi: (bi, 0, 0)),
                  pl.BlockSpec((1, l, 128), lambda bi, qi: (bi, 0, 0)),
                  pl.BlockSpec((1, 128), lambda bi, qi: (0, 0)),
                  pl.BlockSpec((128, 128), lambda bi, qi: (0, 0))],
        out_specs=pl.BlockSpec((1, tq, 256), lambda bi, qi: (bi, qi, 0)),
        out_shape=jax.ShapeDtypeStruct((b, l, 256), F32),
        scratch_shapes=[pltpu.VMEM((8 * tq, LANE), BF16), pltpu.VMEM((l, LANE), BF16),
                        pltpu.VMEM((2, l, LANE), BF16), pltpu.VMEM((8 * tq, tk), BF16),
                        pltpu.VMEM((8 * tq, LANE), F32), pltpu.VMEM((8 * tq, LANE), F32)],
        compiler_params=pltpu.CompilerParams(dimension_semantics=("arbitrary", "arbitrary"),
                                             vmem_limit_bytes=VMEM_LIMIT),
        name="diff_prompt",
    )(lam, qd.reshape(b, l, 256), kd.reshape(b, l, 128), vd.reshape(b, l, 128), lw['gsub'], lw['ones64_128'])


def _fill_mla_q(qbig_ref, q, rows):
    lane = lax.broadcasted_iota(jnp.int32, (1, 2 * LANE), 1)
    for h in range(MLA_HEADS):
        qbig_ref[h * rows:(h + 1) * rows, :] = jnp.where(
            (lane >= h * 64) & (lane < (h + 1) * 64), q, 0.0).astype(BF16)


def _mla_finish(acc_ref, l_ref, wuv_ref, rows):
    o = jnp.zeros((rows, 2 * LANE), F32)
    for h in range(MLA_HEADS):
        den = jnp.sum(l_ref[h * rows:(h + 1) * rows, :], axis=-1, keepdims=True)
        olat = acc_ref[h * rows:(h + 1) * rows, :] / den
        o = o + _dot(olat.astype(BF16), wuv_ref[h])
    return o


def _mlap_kernel(tq, tk, q_ref, k_ref, c_ref, wuv_ref, o_ref, qbig_ref, kb_ref, cb_ref, p_ref, m_ref, l_ref,
                 acc_ref):
    qi = pl.program_id(1)

    @pl.when(qi == 0)
    def _():
        kb_ref[...] = k_ref[0].astype(BF16)
        cb_ref[...] = c_ref[0].astype(BF16)

    _fill_mla_q(qbig_ref, q_ref[0], tq)
    m_ref[...] = jnp.full(m_ref.shape, NEG, F32)
    l_ref[...] = jnp.zeros(l_ref.shape, F32)
    acc_ref[...] = jnp.zeros(acc_ref.shape, F32)
    c1 = MLA_DQK ** -0.5 * LOG2E
    qpos = qi * tq + lax.broadcasted_iota(jnp.int32, (tq, 1), 0)

    def kv_step(j, masked):
        s_all = _dot_nt(qbig_ref[...], kb_ref[pl.ds(j * tk, tk), :])
        kpos = j * tk + lax.broadcasted_iota(jnp.int32, (1, tk), 1)
        for h in range(MLA_HEADS):
            rs = slice(h * tq, (h + 1) * tq)
            t = s_all[rs, :] * c1
            if masked:
                t = jnp.where(kpos <= qpos, t, NEG)
            alpha, ps = _online_update(t, m_ref, h * tq, tq)
            acc_ref[rs, :] = alpha * acc_ref[rs, :]
            lsum = alpha * l_ref[rs, :]
            for c, p in enumerate(ps):
                lsum = lsum + p
                p_ref[rs, c * LANE:(c + 1) * LANE] = p.astype(BF16)
            l_ref[rs, :] = lsum
        acc_ref[...] = acc_ref[...] + _dot(p_ref[...], cb_ref[pl.ds(j * tk, tk), :])

    n_full = (qi * tq) // tk
    lax.fori_loop(0, n_full, lambda j, c: (kv_step(j, False), c)[1], 0)
    kv_step(n_full, True)
    o_ref[0] = _mla_finish(acc_ref, l_ref, wuv_ref, tq)


def _mla_prompt(qm, kmla, c, lw, b, l):
    tq = min(256, l)
    tk = min(256, l)
    assert tk % tq == 0 and l % tk == 0
    nr = MLA_HEADS * tq
    return pl.pallas_call(
        functools.partial(_mlap_kernel, tq, tk),
        grid=(b, l // tq),
        in_specs=[pl.BlockSpec((1, tq, 256), lambda bi, qi: (bi, qi, 0)),
                  pl.BlockSpec((1, l, 256), lambda bi, qi: (bi, 0, 0)),
                  pl.BlockSpec((1, l, 128), lambda bi, qi: (bi, 0, 0)),
                  pl.BlockSpec((MLA_HEADS, 128, 256), lambda bi, qi: (0, 0, 0))],
        out_specs=pl.BlockSpec((1, tq, 256), lambda bi, qi: (bi, qi, 0)),
        out_shape=jax.ShapeDtypeStruct((b, l, 256), F32),
        scratch_shapes=[pltpu.VMEM((nr, 2 * LANE), BF16), pltpu.VMEM((l, 2 * LANE), BF16),
                        pltpu.VMEM((l, LANE), BF16), pltpu.VMEM((nr, tk), BF16),
                        pltpu.VMEM((nr, LANE), F32), pltpu.VMEM((nr, LANE), F32), pltpu.VMEM((nr, LANE), F32)],
        compiler_params=pltpu.CompilerParams(dimension_semantics=("arbitrary", "arbitrary"),
                                             vmem_limit_bytes=VMEM_LIMIT),
        name="mla_prompt",
    )(qm.reshape(b, l, 256), kmla.reshape(b, l, 256), c.reshape(b, l, 128), lw['wuv'])


def _sample_attn_kernel(ppc, nch, n_pages, ns, past_len, lam_init, layer, pt_ref, lam_ref,
                        qd_ref, qabs_ref, qr_ref, kn_ref, vn_ref, cn_ref, krn_ref,
                        wukt_ref, ones8_ref, wuv_ref, gsub_ref, ones64_ref,
                        ckt_hbm, cvt_hbm, cc_hbm, ckrt_hbm, od_ref, om_ref,
                        kbuf, vbuf, cbuf, krbuf, sems,
                        qd_s, qw_s, qr_s, m1, l1, acc1, m2, l2, acc2, pad_s):
    b = pl.program_id(0)
    last_chunk = pl.num_programs(0) * nch - 1
    nd = 8 * ns
    nm = MLA_HEADS * ns
    c1d = DIFF_DH ** -0.5 * LOG2E
    c1m = MLA_DQK ** -0.5 * LOG2E

    def page_copies(chunk, slot):
        base = chunk * ppc
        copies = []
        for j in range(ppc):
            pg = pt_ref[base + j]
            tok = pl.ds(j * PAGE, PAGE)
            for a, (src, dst) in enumerate(((ckt_hbm, kbuf.at[slot, :, tok]), (cvt_hbm, vbuf.at[slot, :, tok]),
                                            (cc_hbm, cbuf.at[slot, tok, :]), (ckrt_hbm, krbuf.at[slot, :, tok]))):
                copies.append(pltpu.make_async_copy(src.at[layer, pg], dst, sems.at[slot, a, j]))
        return copies

    @pl.when(b == 0)
    def _():
        for cp in page_copies(0, 0):
            cp.start()

    _fill_diff_q(qd_s, qd_ref[...], ns)
    for h in range(MLA_HEADS):
        qw_s[h * ns:(h + 1) * ns, :] = qabs_ref[:, h * LANE:(h + 1) * LANE].astype(BF16)
        qr_s[h * ns:(h + 1) * ns, :] = qr_ref[:, h * LANE:(h + 1) * LANE].astype(BF16)
    qw_s[nm:nm + LANE, :] = wukt_ref[...]
    m1[...] = jnp.full(m1.shape, NEG, F32)
    l1[...] = jnp.zeros(l1.shape, F32)
    acc1[...] = jnp.zeros(acc1.shape, F32)
    m2[...] = jnp.full(m2.shape, NEG, F32)
    l2[...] = jnp.zeros(l2.shape, F32)
    acc2[...] = jnp.zeros(acc2.shape, F32)

    rowd = lax.broadcasted_iota(jnp.int32, (nd, 1), 0)
    slope2 = jnp.zeros((nd, 1), F32)
    for r, (kvh, g, mp) in enumerate(_diff_rows()):
        slope2 = jnp.where((rowd >= r * ns) & (rowd < (r + 1) * ns), _diff_slope(g, kvh) * LOG2E, slope2)
    qrow_d = rowd & (ns - 1)
    qrow_m = lax.broadcasted_iota(jnp.int32, (nm, 1), 0) & (ns - 1)

    def mla_scores(r, rope_part, krsq):
        num = r[0:nm, :] + rope_part
        kt2 = r[nm:nm + LANE, :]
        sq = kt2 * kt2
        rows = []
        for h in range(MLA_HEADS):
            ssq = jnp.sum(sq[h * MLA_NOPE:(h + 1) * MLA_NOPE, :], axis=0, keepdims=True) + krsq
            inv = lax.rsqrt(ssq * (1.0 / MLA_DQK) + EPS) * c1m
            rows.append(num[h * ns:(h + 1) * ns, :] * inv)
        return jnp.concatenate(rows, axis=0)

    def accumulate(t, m_ref, l_ref, acc_ref, pv):
        alpha, ps = _online_update(t, m_ref, 0, t.shape[0])
        lsum = alpha * l_ref[...]
        for p in ps:
            lsum = lsum + p
        l_ref[...] = lsum
        p_all = ps[0] if len(ps) == 1 else jnp.concatenate(ps, axis=1)
        acc_ref[...] = alpha * acc_ref[...] + pv(p_all.astype(BF16))

    def chunk_step(ch, slot):
        chunk = b * nch + ch
        for cp in page_copies(chunk, slot):
            cp.wait()
        for cp in page_copies(jnp.minimum(chunk + 1, last_chunk), 1 - slot):
            cp.start()
        n = ppc * PAGE
        kposf = (ch * n + lax.broadcasted_iota(jnp.int32, (1, n), 1)).astype(F32)
        t_d = _dot(qd_s[...], kbuf[slot].astype(BF16)) * c1d + slope2 * kposf
        accumulate(t_d, m1, l1, acc1, lambda p: _dot_nt(p, vbuf[slot].astype(BF16)))
        cb = cbuf[slot].astype(BF16)
        krt = krbuf[slot]
        rope_part = _dot(qr_s[:, 0:MLA_ROPE], krt.astype(BF16))
        t_m = mla_scores(_dot_nt(qw_s[...], cb), rope_part, jnp.sum(krt * krt, axis=0, keepdims=True))
        accumulate(t_m, m2, l2, acc2, lambda p: _dot(p, cb))

    def chunk_pair(k, carry):
        chunk_step(2 * k, 0)
        chunk_step(2 * k + 1, 1)
        return carry

    lax.fori_loop(0, nch // 2, chunk_pair, 0)

    jpos = lax.broadcasted_iota(jnp.int32, (1, PAGE), 1)
    pad_s[...] = jnp.zeros(pad_s.shape, F32)
    pad_s[0, 0:ns, :] = kn_ref[...]
    pad_s[1, 0:ns, :] = vn_ref[...]
    pad_s[2, 0:ns, :] = cn_ref[...]
    pad_s[3, 0:ns, :] = krn_ref[...]
    knb = pad_s[0].astype(BF16)
    vnb = pad_s[1].astype(BF16)
    cnb = pad_s[2].astype(BF16)
    krn = pad_s[3]
    t = _dot_nt(qd_s[...], knb) * c1d + slope2 * (past_len + jpos).astype(F32)
    accumulate(jnp.where(jpos <= qrow_d, t, NEG), m1, l1, acc1, lambda p: _dot(p, vnb))
    hi, lo = _split2(krn * krn)
    krsq = (_dot_nt(ones8_ref[...], hi) + _dot_nt(ones8_ref[...], lo))[0:1, :]
    t = mla_scores(_dot_nt(qw_s[...], cnb), _dot_nt(qr_s[...], krn.astype(BF16)), krsq)
    accumulate(jnp.where(jpos <= qrow_m, t, NEG), m2, l2, acc2, lambda p: _dot(p, cnb))

    def o_of(r):
        den = jnp.sum(l1[r * ns:(r + 1) * ns, :], axis=-1, keepdims=True)
        return acc1[r * ns:(r + 1) * ns, :] / den

    outs = _diff_finish(o_of, lam_ref[0], gsub_ref[...], ones64_ref[...], lam_init)
    od_ref[:, 0:LANE] = outs[0]
    od_ref[:, LANE:2 * LANE] = outs[1]
    om_ref[...] = _mla_finish(acc2, l2, wuv_ref, ns)

    @pl.when(b == pl.num_programs(0) - 1)
    def _():
        for cp in page_copies(last_chunk, 0):
            cp.wait()


def _sample_attn(layer, lam, lw, lam_init, page_table, caches, qd, qabs, qr, kn, vn, cn, krn, nb, ns):
    n_pages = page_table.shape[1]
    ppc = math.gcd(n_pages, 32)
    nch = n_pages // ppc
    assert ns & (ns - 1) == 0 and nch % 2 == 0
    past_len = n_pages * PAGE
    pt_flat = page_table.reshape(-1)
    seq = lambda w: pl.BlockSpec((ns, w), lambda i, pt: (i, 0))
    const = lambda shape: pl.BlockSpec(shape, lambda i, pt: (0,) * len(shape))
    hbm = pl.BlockSpec(memory_space=pl.ANY)
    in_specs = [pl.BlockSpec(memory_space=pltpu.SMEM),
                seq(256), seq(512), seq(512), seq(128), seq(128), seq(128), seq(128),
                const((128, 128)), const((8, 128)), const((MLA_HEADS, 128, 256)),
                const((1, 128)), const((128, 128)), hbm, hbm, hbm, hbm]
    ins = [lam, qd, qabs, qr, kn, vn, cn, krn, lw['wukt'], lw['ones8'], lw['wuv'], lw['gsub'],
           lw['ones64_128'], *caches]
    nd, nm = 8 * ns, MLA_HEADS * ns
    kern = functools.partial(_sample_attn_kernel, ppc, nch, n_pages, ns, past_len, lam_init, layer)
    n = ppc * PAGE
    wide_buf = lambda rows: pltpu.VMEM((2, rows, n), F32)
    return pl.pallas_call(
        kern,
        grid_spec=pltpu.PrefetchScalarGridSpec(
            num_scalar_prefetch=1,
            grid=(nb,),
            in_specs=in_specs,
            out_specs=[seq(256), seq(256)],
            scratch_shapes=[wide_buf(PAGE), wide_buf(PAGE), pltpu.VMEM((2, n, LANE), F32), wide_buf(MLA_ROPE),
                            pltpu.SemaphoreType.DMA((2, 4, ppc)),
                            pltpu.VMEM((nd, LANE), BF16), pltpu.VMEM((nm + LANE, LANE), BF16),
                            pltpu.VMEM((nm, LANE), BF16),
                            pltpu.VMEM((nd, LANE), F32), pltpu.VMEM((nd, LANE), F32), pltpu.VMEM((nd, LANE), F32),
                            pltpu.VMEM((nm, LANE), F32), pltpu.VMEM((nm, LANE), F32), pltpu.VMEM((nm, LANE), F32),
                            pltpu.VMEM((4, PAGE, LANE), F32)]),
        out_shape=[jax.ShapeDtypeStruct((nb * ns, 256), F32), jax.ShapeDtypeStruct((nb * ns, 256), F32)],
        compiler_params=pltpu.CompilerParams(dimension_semantics=("arbitrary",), vmem_limit_bytes=VMEM_LIMIT),
        name="sample_attn",
    )(pt_flat, *ins)


N_ST = 2 * SSM_W // LANE


def _s5_pitch(tb):
    return tb + SUB if (tb // SUB) % 2 == 0 else tb


def _s5_kernel(tb, u_ref, h0re_ref, h0im_ref, bbar_ref, are_ref, aim_ref, cmat_ref, d_ref, wglu_ref,
               o_ref, hre_ref, him_ref, ubt, utm, bus, hs, otm, obt, hst):
    tblk = pl.program_id(1)
    pitch = _s5_pitch(tb)
    half = N_ST // 2
    nlt = BRANCH_W // LANE

    @pl.when(tblk == 0)
    def _():
        for j in range(half):
            hst[j] = h0re_ref[:, j * LANE:(j + 1) * LANE]
            hst[half + j] = h0im_ref[:, j * LANE:(j + 1) * LANE]

    for bi in range(SUB):
        for c in range(nlt):
            ubt[c, bi * pitch:bi * pitch + tb, :] = u_ref[bi, :, c * LANE:(c + 1) * LANE]

    def to_time_major(t, carry):
        r0 = pl.multiple_of(t * SUB, SUB)
        for c in range(nlt):
            utm[pl.ds(r0, SUB), c * LANE:(c + 1) * LANE] = ubt[c, pl.ds(t, SUB, stride=pitch), :]
        return carry

    lax.fori_loop(0, tb, to_time_major, 0)
    bus[...] = _dot(utm[...].astype(BF16), bbar_ref[...])
    ar = [jnp.broadcast_to(are_ref[:, j * LANE:(j + 1) * LANE], (SUB, LANE)) for j in range(half)]
    ai = [jnp.broadcast_to(aim_ref[:, j * LANE:(j + 1) * LANE], (SUB, LANE)) for j in range(half)]

    def step(t, st):
        r0 = pl.multiple_of(t * SUB, SUB)
        new = [None] * N_ST
        for j in range(half):
            hr, hi = st[j], st[half + j]
            nr = ar[j] * hr - ai[j] * hi + bus[pl.ds(r0, SUB), j * LANE:(j + 1) * LANE]
            ni = ar[j] * hi + ai[j] * hr + bus[pl.ds(r0, SUB), (half + j) * LANE:(half + j + 1) * LANE]
            hs[pl.ds(r0, SUB), j * LANE:(j + 1) * LANE] = nr
            hs[pl.ds(r0, SUB), (half + j) * LANE:(half + j + 1) * LANE] = ni
            new[j], new[half + j] = nr, ni
        return tuple(new)

    st = lax.fori_loop(0, tb, step, tuple(hst[j] for j in range(N_ST)))
    for j in range(N_ST):
        hst[j] = st[j]

    y = d_ref[...] * utm[...] + _dot(hs[...].astype(BF16), cmat_ref[...])
    yg = _dot(y.astype(BF16), wglu_ref[...])
    otm[...] = yg[:, :BRANCH_W] * _sigmoid(yg[:, BRANCH_W:])

    def to_batch_major(t, carry):
        r0 = pl.multiple_of(t * SUB, SUB)
        for c in range(nlt):
            obt[c, pl.ds(t, SUB, stride=pitch), :] = otm[pl.ds(r0, SUB), c * LANE:(c + 1) * LANE]
        return carry

    lax.fori_loop(0, tb, to_batch_major, 0)
    for bi in range(SUB):
        for c in range(nlt):
            o_ref[bi, :, c * LANE:(c + 1) * LANE] = obt[c, bi * pitch:bi * pitch + tb, :]

    @pl.when(tblk == pl.num_programs(1) - 1)
    def _():
        for j in range(half):
            hre_ref[:, j * LANE:(j + 1) * LANE] = st[j]
            him_ref[:, j * LANE:(j + 1) * LANE] = st[half + j]


def _s5(su, h0re, h0im, lw, b, l):
    assert b % SUB == 0
    tb = min(64, l)
    rows = SUB * tb
    const = lambda shape: pl.BlockSpec(shape, lambda bi, ti: (0,) * len(shape))
    st_spec = pl.BlockSpec((SUB, SSM_W), lambda bi, ti: (bi, 0))
    return pl.pallas_call(
        functools.partial(_s5_kernel, tb),
        grid=(b // SUB, l // tb),
        in_specs=[pl.BlockSpec((SUB, tb, BRANCH_W), lambda bi, ti: (bi, ti, 0)), st_spec, st_spec,
                  const((BRANCH_W, 2 * SSM_W)), const((1, SSM_W)), const((1, SSM_W)),
                  const((2 * SSM_W, BRANCH_W)), const((1, BRANCH_W)), const((BRANCH_W, 2 * BRANCH_W))],
        out_specs=[pl.BlockSpec((SUB, tb, BRANCH_W), lambda bi, ti: (bi, ti, 0)), st_spec, st_spec],
        out_shape=[jax.ShapeDtypeStruct((b, l, BRANCH_W), F32), jax.ShapeDtypeStruct((b, SSM_W), F32),
                   jax.ShapeDtypeStruct((b, SSM_W), F32)],
        scratch_shapes=[pltpu.VMEM((BRANCH_W // LANE, SUB * _s5_pitch(tb), LANE), F32),
                        pltpu.VMEM((rows, BRANCH_W), F32),
                        pltpu.VMEM((rows, 2 * SSM_W), F32), pltpu.VMEM((rows, 2 * SSM_W), F32),
                        pltpu.VMEM((rows, BRANCH_W), F32),
                        pltpu.VMEM((BRANCH_W // LANE, SUB * _s5_pitch(tb), LANE), F32),
                        pltpu.VMEM((N_ST, SUB, LANE), F32)],
        compiler_params=pltpu.CompilerParams(dimension_semantics=("arbitrary", "arbitrary"),
                                             vmem_limit_bytes=VMEM_LIMIT),
        name="s5",
    )(su.reshape(b, l, BRANCH_W), h0re, h0im, lw['bbar'], lw['a_re'], lw['a_im'], lw['cmat'], lw['ssm_d'], lw['wglu'])


def _hgrn_kernel(chunk, nchunk, chain, q_ref, f_ref, v_ref, s0_ref, lb_ref, gain_ref, tri_ref, same_ref,
                 ones64_ref, o_ref, sfin_ref, kp, bcp, vp, st, oacc):
    rows = chunk * nchunk
    q = q_ref[0]
    zf = f_ref[0]
    v = v_ref[0]
    lb = lb_ref[...]
    logf = jnp.log(lb + (1.0 - lb) * _sigmoid(zf))
    kk = (1.0 - lb) * _sigmoid(-zf)
    bc = _dot3t(tri_ref[...], logf)
    bl = _dot3t(same_ref[...], logf)
    qh = q * jnp.exp(bc)
    kdec = (kk * jnp.exp(bl - bc)).astype(BF16)

    kp[0:HG_CHUNK, :] = jnp.zeros((HG_CHUNK, BRANCH_W), F32)
    bcp[0:HG_CHUNK, :] = jnp.zeros((HG_CHUNK, BRANCH_W), F32)
    vp[0:HG_CHUNK, :] = jnp.zeros((HG_CHUNK, BRANCH_W), F32)
    kp[HG_CHUNK:HG_CHUNK + rows, :] = kk
    bcp[HG_CHUNK:HG_CHUNK + rows, :] = bc
    vp[HG_CHUNK:HG_CHUNK + rows, :] = v
    rpos = lax.broadcasted_iota(jnp.int32, (rows, 1), 0) & (chunk - 1)
    o = _dot((q * kk).astype(BF16), ones64_ref[...]) * v
    for d in range(1, chunk):
        lo = HG_CHUNK - d
        ks = kp[lo:lo + rows, :]
        bcs = bcp[lo:lo + rows, :]
        vs = vp[lo:lo + rows, :]
        x = q * ks * jnp.exp(jnp.where(rpos >= d, bc - bcs, NEG))
        o = o + _dot(x.astype(BF16), ones64_ref[...]) * vs
    oacc[...] = o

    lane = lax.broadcasted_iota(jnp.int32, (1, BRANCH_W), 1)
    rowi = lax.broadcasted_iota(jnp.int32, (rows, 1), 0)
    head_lanes = [(lane >= h * HG_DK) & (lane < (h + 1) * HG_DK) for h in range(HG_HEADS)]
    kvs = []
    for i in range(nchunk):
        r0 = i * chunk
        vmask = jnp.where((rowi >= r0) & (rowi < r0 + chunk), v, 0.0).astype(BF16)
        kvt = _dot_tn(vmask, kdec)
        comp = jnp.zeros((HG_DV, BRANCH_W), F32)
        for h in range(HG_HEADS):
            comp = jnp.where(head_lanes[h], kvt[h * HG_DV:(h + 1) * HG_DV, :], comp)
        kvs.append(comp)
    if chain:
        @pl.when(pl.program_id(1) == 0)
        def _():
            st[...] = s0_ref[0]
        s_t = st[...]
    for i in range(nchunk):
        r0 = i * chunk
        if not chain:
            s_t = s0_ref[i]
        full = jnp.concatenate([jnp.where(head_lanes[h], s_t, 0.0) for h in range(HG_HEADS)],
                               axis=0).astype(BF16)
        oacc[r0:r0 + chunk, :] = oacc[r0:r0 + chunk, :] + _dot_nt(qh[r0:r0 + chunk, :].astype(BF16), full)
        s_t = s_t * jnp.exp(bl[r0:r0 + 1, :]) + kvs[i]
        if not chain:
            sfin_ref[i] = s_t
    if chain:
        st[...] = s_t
        sfin_ref[0] = s_t
    ot = oacc[...]
    msq = _dot2(ot * ot, ones64_ref[...]) * (1.0 / HG_DV)
    o_ref[0] = ot * lax.rsqrt(msq + EPS) * gain_ref[...]


def _dot3t(m, x):
    hi, mid, lo = _split3(x)
    return _dot(m, hi) + _dot(m, mid) + _dot(m, lo)


def _hgrn(hq, hf, hi, s0t, lw, b, l):
    chunk = math.gcd(l, HG_CHUNK)
    if l >= PAGE:
        chain, rows, nblk, outer = True, PAGE, l // PAGE, b
    else:
        assert l == chunk
        per = min(b, PAGE // l)
        chain, rows, nblk, outer = False, per * l, 1, b // per
    nchunk = rows // chunk
    sblk = 1 if chain else nchunk
    r = np.arange(rows)
    same = (r[:, None] // chunk) == (r[None, :] // chunk)
    tri = same & (r[None, :] <= r[:, None])
    blk = lambda: pl.BlockSpec((1, rows, BRANCH_W), lambda bi, ti: (bi * nblk + ti, 0, 0))
    const = lambda shape: pl.BlockSpec(shape, lambda bi, ti: (0,) * len(shape))
    st_spec = pl.BlockSpec((sblk, HG_DV, BRANCH_W), lambda bi, ti: (bi, 0, 0))
    shp = (outer * nblk, rows, BRANCH_W)
    o, sfin = pl.pallas_call(
        functools.partial(_hgrn_kernel, chunk, nchunk, chain),
        grid=(outer, nblk),
        in_specs=[blk(), blk(), blk(), st_spec, const((1, BRANCH_W)), const((1, BRANCH_W)),
                  const((rows, rows)), const((rows, rows)), const((BRANCH_W, BRANCH_W))],
        out_specs=[blk(), st_spec],
        out_shape=[jax.ShapeDtypeStruct(shp, F32), jax.ShapeDtypeStruct((b, HG_DV, BRANCH_W), F32)],
        scratch_shapes=[pltpu.VMEM((HG_CHUNK + rows, BRANCH_W), F32), pltpu.VMEM((HG_CHUNK + rows, BRANCH_W), F32),
                        pltpu.VMEM((HG_CHUNK + rows, BRANCH_W), F32), pltpu.VMEM((HG_DV, BRANCH_W), F32),
                        pltpu.VMEM((rows, BRANCH_W), F32)],
        compiler_params=pltpu.CompilerParams(dimension_semantics=("arbitrary", "arbitrary"),
                                             vmem_limit_bytes=VMEM_LIMIT),
        name="hgrn_chain" if chain else "hgrn_step",
    )(hq.reshape(shp), hf.reshape(shp), hi.reshape(shp), s0t, lw['lb'], lw['hg_gain'],
      jnp.asarray(tri, BF16), jnp.asarray(same, BF16), lw['ones64'])
    return o.reshape(b * l, BRANCH_W), sfin


def _block_ones(n, blk):
    r = np.arange(n)
    return jnp.asarray((r[:, None] // blk) == (r[None, :] // blk), BF16)


def _layout_indices():
    zero = IN_COLS
    a = np.full((NA,), zero, np.int64)
    for g in range(2):
        for kvh in range(2):
            for mp in range(2):
                dst = _A['dq'] + g * 128 + kvh * 64 + mp * 32
                src = _OFF['dq'] + kvh * 128 + g * 64 + mp * 32
                a[dst:dst + 32] = np.arange(src, src + 32)
    a[_A['dk']:_A['dk'] + 128] = np.arange(_OFF['dk'], _OFF['dk'] + 128)
    a[_A['dv']:_A['dv'] + 128] = np.arange(_OFF['dv'], _OFF['dv'] + 128)
    a[_A['mqa']:_A['mqa'] + MLA_QL] = np.arange(_OFF['mqa'], _OFF['mqa'] + MLA_QL)
    a[_A['mkva']:_A['mkva'] + 128] = np.arange(_OFF['mkva'], _OFF['mkva'] + 128)
    for h in range(MLA_HEADS):
        dst = _A['krt'] + h * 64 + MLA_NOPE
        a[dst:dst + MLA_ROPE] = np.arange(_OFF['mkr'], _OFF['mkr'] + MLA_ROPE)
    a[_A['kr0']:_A['kr0'] + MLA_ROPE] = np.arange(_OFF['mkr'], _OFF['mkr'] + MLA_ROPE)
    for name in ('su', 'hq', 'hf', 'hi'):
        a[_A[name]:_A[name] + 256] = np.arange(_OFF[name], _OFF[name] + 256)
    perm = np.zeros((256,), np.int64)
    for g in range(2):
        for kvh in range(2):
            perm[g * 128 + kvh * 64:g * 128 + kvh * 64 + 64] = np.arange(kvh * 128 + g * 64, kvh * 128 + g * 64 + 64)
    bcols = np.concatenate([_OFF['gates'] + perm, np.arange(_OFF['gates'] + 256, _OFF['gates'] + 1024),
                            np.arange(_OFF['merge'], _OFF['merge'] + N_BRANCH * D_MODEL)])
    return a, bcols, perm


_IDX_A, _IDX_B, _PERM_DIFF = _layout_indices()


def _head48(vec):
    return jnp.tile(jnp.pad(vec, (0, 64 - MLA_DQK)), MLA_HEADS).reshape(1, 256)


def _prep_layer(p, l, lb_all):
    w_in = jnp.pad(p['w_in'][l], ((0, 0), (0, 1)))
    lw = {}
    lw['g'] = p['norm_gain'][l].reshape(1, D_MODEL)
    lw['wa'] = jnp.take(w_in, _IDX_A, axis=1).astype(BF16)
    lw['wb'] = jnp.take(w_in, _IDX_B, axis=1).astype(BF16)
    wbr = p['w_branch'][l]
    lw['wbr'] = jnp.concatenate([jnp.take(wbr[0], _PERM_DIFF, axis=0)[None], wbr[1:]], axis=0).astype(BF16)
    lw['wout'] = p['w_out'][l].astype(BF16)
    lw['gq'] = jnp.tile(p['diff_q_gain'][l], 8).reshape(1, 256)
    lw['gk'] = jnp.tile(p['diff_k_gain'][l], 4).reshape(1, 128)
    lw['gsub'] = jnp.tile(p['diff_subln_gain'][l], 2).reshape(1, 128)
    lw['gqa'] = jnp.pad(p['mla_qa_gain'][l], (0, 256 - MLA_QL)).reshape(1, 256)
    wuq = p['w_mla_uq'][l].reshape(MLA_QL, MLA_HEADS, MLA_DQK)
    lw['wuq'] = jnp.pad(wuq, ((0, 256 - MLA_QL), (0, 0), (0, 64 - MLA_DQK))).reshape(256, 256).astype(BF16)
    lw['gq48'] = _head48(p['mla_q_gain'][l])
    lw['gk48'] = _head48(p['mla_k_gain'][l])
    lw['gkva'] = p['mla_kva_gain'][l].reshape(1, 128)
    wuk = p['w_mla_uk'][l]
    lw['wukp'] = jnp.pad(wuk, ((0, 0), (0, 0), (0, 64 - MLA_NOPE))).reshape(128, 256).astype(BF16)
    lw['wuk2d'] = wuk.reshape(128, MLA_HEADS * MLA_NOPE).astype(BF16)
    wabs = jnp.zeros((MLA_HEADS, 64, MLA_HEADS, 128), F32)
    selr = np.zeros((MLA_HEADS, 64, MLA_HEADS, 128), np.float32)
    for h in range(MLA_HEADS):
        wabs = wabs.at[h, :MLA_NOPE, h, :].set(wuk[:, h, :].T)
        selr[h, MLA_NOPE + np.arange(MLA_ROPE), h, np.arange(MLA_ROPE)] = 1.0
    lw['wabs'] = wabs.reshape(256, 512).astype(BF16)
    lw['selr'] = jnp.asarray(selr.reshape(256, 512), BF16)
    wuv = p['w_mla_uv'][l]
    wuvp = jnp.zeros((MLA_HEADS, 128, MLA_HEADS, MLA_DV), F32)
    for h in range(MLA_HEADS):
        wuvp = wuvp.at[h, :, h, :].set(wuv[:, h, :])
    lw['wuv'] = wuvp.reshape(MLA_HEADS, 128, 256).astype(BF16)
    lw['wukt'] = wuk.reshape(128, MLA_HEADS * MLA_NOPE).T.astype(BF16)
    ones8 = np.zeros((8, 128), np.float32)
    ones8[:, :MLA_ROPE] = 1.0
    lw['ones8'] = jnp.asarray(ones8, BF16)
    lw['ones32'] = _block_ones(256, 32)
    lw['ones64'] = _block_ones(256, 64)
    lw['ones64_128'] = _block_ones(128, 64)

    lre, lim = p['ssm_a_re'][l], p['ssm_a_im'][l]
    dt = jnp.exp(p['ssm_log_dt'][l])[:, None]
    mag = jnp.exp(lre * dt)
    are, aim = mag * jnp.cos(lim * dt), mag * jnp.sin(lim * dt)
    den = lre * lre + lim * lim
    cre = ((are - 1.0) * lre + aim * lim) / den
    cim = (aim * lre - (are - 1.0) * lim) / den
    bre, bim = p['ssm_b_re'][l], p['ssm_b_im'][l]
    bbre = cre[..., None] * bre - cim[..., None] * bim
    bbim = cre[..., None] * bim + cim[..., None] * bre
    eye = jnp.eye(SSM_GROUPS, dtype=F32)
    b_re = jnp.einsum('gph,gk->ghkp', bbre, eye).reshape(BRANCH_W, SSM_W)
    b_im = jnp.einsum('gph,gk->ghkp', bbim, eye).reshape(BRANCH_W, SSM_W)
    lw['bbar'] = jnp.concatenate([b_re, b_im], axis=1).astype(BF16)
    c_re = jnp.einsum('ghp,gk->gpkh', p['ssm_c_re'][l], eye).reshape(SSM_W, BRANCH_W)
    c_im = jnp.einsum('ghp,gk->gpkh', p['ssm_c_im'][l], eye).reshape(SSM_W, BRANCH_W)
    lw['cmat'] = jnp.concatenate([c_re, -c_im], axis=0).astype(BF16)
    lw['a_re'] = are.reshape(1, SSM_W)
    lw['a_im'] = aim.reshape(1, SSM_W)
    lw['ssm_d'] = p['ssm_d'][l].reshape(1, BRANCH_W)
    lw['wglu'] = p['w_glu'][l].astype(BF16)
    lw['lb'] = lb_all[l].reshape(1, BRANCH_W)
    lw['hg_gain'] = jnp.tile(p['hg_norm_gain'][l], HG_HEADS).reshape(1, BRANCH_W)
    lp = p['diff_lambda'][l]
    lam_init = 0.8 - 0.6 * math.exp(-0.3 * l)
    lw['lam'] = (jnp.exp(jnp.sum(lp[0] * lp[1])) - jnp.exp(jnp.sum(lp[2] * lp[3])) + lam_init).reshape(1)
    return lw, lam_init


def _rope_tables(pos):
    half = MLA_ROPE // 2
    freqs = ROPE_BASE ** (-jnp.arange(half, dtype=F32) / half)
    ang = pos.astype(F32)[:, None] * freqs
    cos, sin = jnp.cos(ang), jnp.sin(ang)
    n = pos.shape[0]
    one = jnp.ones((n, MLA_NOPE), F32)
    zero = jnp.zeros((n, MLA_NOPE), F32)
    pad1 = jnp.ones((n, 64 - MLA_DQK), F32)
    pad0 = jnp.zeros((n, 64 - MLA_DQK), F32)
    cs_h = jnp.concatenate([one, cos, cos, pad1], axis=1)
    sn_h = jnp.concatenate([zero, -sin, sin, pad0], axis=1)
    cs0 = jnp.concatenate([cos, cos, jnp.ones((n, LANE - MLA_ROPE), F32)], axis=1)
    sn0 = jnp.concatenate([-sin, sin, jnp.zeros((n, LANE - MLA_ROPE), F32)], axis=1)
    return dict(cs=jnp.tile(cs_h, (1, 2)), sn=jnp.tile(sn_h, (1, 2)), cs0=cs0, sn0=sn0)


def kernel(x_prompt, x_sample, cache_diff_k, cache_diff_v, cache_mla_c, cache_mla_kr, state_ssm_re, state_ssm_im, state_hgrn, page_table, norm_gain, w_in, w_branch, w_out, diff_q_gain, diff_k_gain, diff_lambda, diff_subln_gain, mla_qa_gain, mla_kva_gain, w_mla_uq, w_mla_uk, w_mla_uv, mla_q_gain, mla_k_gain, ssm_a_re, ssm_a_im, ssm_log_dt, ssm_b_re, ssm_b_im, ssm_c_re, ssm_c_im, ssm_d, w_glu, hg_lb_logits, hg_norm_gain):
    p = dict(norm_gain=norm_gain, w_in=w_in, w_branch=w_branch, w_out=w_out, diff_q_gain=diff_q_gain,
             diff_k_gain=diff_k_gain, diff_lambda=diff_lambda, diff_subln_gain=diff_subln_gain,
             mla_qa_gain=mla_qa_gain, mla_kva_gain=mla_kva_gain, w_mla_uq=w_mla_uq, w_mla_uk=w_mla_uk,
             w_mla_uv=w_mla_uv, mla_q_gain=mla_q_gain, mla_k_gain=mla_k_gain, ssm_a_re=ssm_a_re, ssm_a_im=ssm_a_im,
             ssm_log_dt=ssm_log_dt, ssm_b_re=ssm_b_re, ssm_b_im=ssm_b_im, ssm_c_re=ssm_c_re, ssm_c_im=ssm_c_im,
             ssm_d=ssm_d, w_glu=w_glu, hg_norm_gain=hg_norm_gain)
    depth = w_in.shape[0]
    bp, lp, _ = x_prompt.shape
    bs, ls, _ = x_sample.shape
    n_pages = page_table.shape[1]
    past_len = n_pages * PAGE
    tp, ts = bp * lp, bs * ls

    sm = jax.nn.softmax(hg_lb_logits.astype(F32), axis=0)
    lb_all = jnp.cumsum(sm, axis=0) - sm[0]

    tm_p = min(512, lp)
    tm_s = min(512, ts)
    assert tm_s % ls == 0
    tabs_p = _rope_tables(jnp.arange(lp))
    tabs_s = _rope_tables(past_len + (jnp.arange(tm_s) % ls))

    n_pool = cache_diff_k.shape[1]
    ckt = jnp.transpose(cache_diff_k, (0, 1, 3, 4, 2)).reshape(depth, n_pool, 128, PAGE)
    cvt = jnp.transpose(cache_diff_v, (0, 1, 3, 4, 2)).reshape(depth, n_pool, 128, PAGE)
    ckrt = jnp.transpose(cache_mla_kr, (0, 1, 3, 2))
    caches = (ckt, cvt, cache_mla_c, ckrt)

    hp = x_prompt.reshape(tp, D_MODEL)
    hs = x_sample.reshape(ts, D_MODEL)
    zeros_p = jnp.zeros((bp, SSM_W), F32)
    zeros_st = jnp.zeros((bp, HG_DV, BRANCH_W), F32)
    outs_p, outs_s = [], []
    for l in range(depth):
        lw, lam_init = _prep_layer(p, l, lb_all)
        (qd, kd, vd, qm, c, kmla, kr16, su, hq, hf, hi) = _inproj(hp, lw, tabs_p, tm_p, lp // tm_p, False)
        o_diff = _diff_prompt(qd, kd, vd, lw['lam'], lw, lam_init, bp, lp).reshape(tp, 256)
        o_mla = _mla_prompt(qm, kmla, c, lw, bp, lp).reshape(tp, 256)
        o_ssm, hre, him = _s5(su, zeros_p, zeros_p, lw, bp, lp)
        o_hg, sfin = _hgrn(hq, hf, hi, zeros_st, lw, bp, lp)
        hp = _merge(hp, lw, (o_diff, o_mla, o_ssm.reshape(tp, 256), o_hg), tm_p)
        outs_p.append((kd, vd, c, kr16, hre, him, sfin))
        (qd, kd, vd, c, kr16, kr128, qabs, qr, su, hq, hf, hi) = _inproj(hs, lw, tabs_s, tm_s, 1, True)
        o_diff, o_mla = _sample_attn(l, lw['lam'], lw, lam_init, page_table, caches,
                                     qd, qabs, qr, kd, vd, c, kr128, bs, ls)
        o_ssm, hre, him = _s5(su, state_ssm_re[l].reshape(bs, SSM_W), state_ssm_im[l].reshape(bs, SSM_W), lw, bs, ls)
        s0t = jnp.transpose(state_hgrn[l], (0, 3, 1, 2)).reshape(bs, HG_DV, BRANCH_W)
        o_hg, sfin = _hgrn(hq, hf, hi, s0t, lw, bs, ls)
        hs = _merge(hs, lw, (o_diff, o_mla, o_ssm.reshape(ts, 256), o_hg), tm_s)
        outs_s.append((kd, vd, c, kr16, hre, him, sfin))

    def stack(group, i):
        return jnp.stack([st[i] for st in group], axis=0)

    def unstate(s, b):
        return jnp.transpose(s.reshape(depth, b, HG_DV, HG_HEADS, HG_DK), (0, 1, 3, 4, 2))

    npg = lp // PAGE
    return (hp.reshape(bp, lp, D_MODEL), hs.reshape(bs, ls, D_MODEL),
            stack(outs_p, 0).reshape(depth, bp, npg, PAGE, DIFF_KVH, 2 * DIFF_DH),
            stack(outs_p, 1).reshape(depth, bp, npg, PAGE, DIFF_KVH, DIFF_DV),
            stack(outs_p, 2).reshape(depth, bp, npg, PAGE, MLA_KVL),
            stack(outs_p, 3).reshape(depth, bp, npg, PAGE, MLA_ROPE),
            stack(outs_p, 4).reshape(depth, bp, SSM_GROUPS, SSM_STATE),
            stack(outs_p, 5).reshape(depth, bp, SSM_GROUPS, SSM_STATE),
            unstate(stack(outs_p, 6), bp),
            stack(outs_s, 0).reshape(depth, bs, ls, DIFF_KVH, 2 * DIFF_DH),
            stack(outs_s, 1).reshape(depth, bs, ls, DIFF_KVH, DIFF_DV),
            stack(outs_s, 2).reshape(depth, bs, ls, MLA_KVL),
            stack(outs_s, 3).reshape(depth, bs, ls, MLA_ROPE),
            stack(outs_s, 4).reshape(depth, bs, SSM_GROUPS, SSM_STATE),
            stack(outs_s, 5).reshape(depth, bs, SSM_GROUPS, SSM_STATE),
            unstate(stack(outs_s, 6), bs))
```

```python
import functools
import math

import numpy as np
import jax
import jax.numpy as jnp
from jax import lax
from jax.experimental import pallas as pl
from jax.experimental.pallas import tpu as pltpu

F32 = jnp.float32
BF16 = jnp.bfloat16
EPS = 1e-6
NEG = -1e30

D_MODEL = 1024
BRANCH_W = 256
N_BRANCH = 4
DIFF_HEADS, DIFF_KVH, DIFF_DH, DIFF_DV = 4, 2, 32, 64
MLA_HEADS, MLA_QL, MLA_KVL, MLA_NOPE, MLA_ROPE, MLA_DV = 4, 192, 128, 32, 16, 64
MLA_DQK = MLA_NOPE + MLA_ROPE
ROPE_BASE = 10000.0
SSM_GROUP, SSM_GROUPS, SSM_STATE = 16, 16, 64
SSM_W = SSM_GROUPS * SSM_STATE
HG_HEADS, HG_DK, HG_DV, HG_CHUNK = 4, 64, 64, 16
PAGE = 128
LANE = 128
SUB = 8
VMEM_LIMIT = 56 * 1024 * 1024

_OFF = dict(dq=0, dk=256, dv=384, mqa=512, mkva=704, mkr=832, su=848, hq=1104, hf=1360, hi=1616,
            gates=1872, merge=2896)
IN_COLS = 6992
_A = dict(dq=0, dk=256, dv=384, mqa=512, mkva=768, krt=896, kr0=1152, su=1280, hq=1536, hf=1792, hi=2048)
NA = 2304


def _dot(a, b):
    return jnp.dot(a, b, preferred_element_type=F32)


def _dot_nt(a, b):
    return lax.dot_general(a, b, (((1,), (1,)), ((), ())), preferred_element_type=F32)


def _dot_tn(a, b):
    return lax.dot_general(a, b, (((0,), (0,)), ((), ())), preferred_element_type=F32)


def _split2(x):
    hi = x.astype(BF16)
    lo = (x - hi.astype(F32)).astype(BF16)
    return hi, lo


def _split3(x):
    hi = x.astype(BF16)
    r = x - hi.astype(F32)
    mid = r.astype(BF16)
    lo = (r - mid.astype(F32)).astype(BF16)
    return hi, mid, lo


def _dot2(x, w):
    hi, lo = _split2(x)
    return _dot(hi, w) + _dot(lo, w)


def _dot3(x, w):
    hi, mid, lo = _split3(x)
    return _dot(hi, w) + _dot(mid, w) + _dot(lo, w)


def _sigmoid(x):
    return 1.0 / (1.0 + jnp.exp(-x))


def _rope_lanes(x, cs, sn, x1mask):
    rot = jnp.where(x1mask, pltpu.roll(x, LANE - MLA_ROPE // 2, 1), pltpu.roll(x, MLA_ROPE // 2, 1))
    return x * cs + rot * sn


def _inproj_kernel(sample, *refs):
    (x_ref, g_ref, wa_ref, gq_ref, gk_ref, gqa_ref, wuq_ref, gq48_ref, gkva_ref, wukp_ref, gk48_ref,
     cs_ref, sn_ref, cs0_ref, sn0_ref, ones32_ref, ones64_ref) = refs[:17]
    if sample:
        wabs_ref, selr_ref = refs[17:19]
        (qd_ref, kd_ref, vd_ref, c_ref, kr16_ref, kr128_ref, qabs_ref, qr_ref,
         su_ref, hq_ref, hf_ref, hi_ref) = refs[19:]
    else:
        (qd_ref, kd_ref, vd_ref, qm_ref, c_ref, kmla_ref, kr16_ref,
         su_ref, hq_ref, hf_ref, hi_ref) = refs[17:]

    x = x_ref[...]
    ms = jnp.mean(x * x, axis=-1, keepdims=True)
    xn = (x * lax.rsqrt(ms + EPS) * g_ref[...]).astype(BF16)
    h = _dot(xn, wa_ref[...])

    def seg(name, width):
        return h[:, _A[name]:_A[name] + width]

    dq = seg('dq', 256)
    msq = _dot2(dq * dq, ones32_ref[...]) * (1.0 / DIFF_DH)
    qd_ref[...] = dq * lax.rsqrt(msq + EPS) * gq_ref[...]
    dk = seg('dk', 128)
    msk = _dot2(dk * dk, ones32_ref[0:128, 0:128]) * (1.0 / DIFF_DH)
    kd_ref[...] = dk * lax.rsqrt(msk + EPS) * gk_ref[...]
    vd_ref[...] = seg('dv', 128)

    lane = lax.broadcasted_iota(jnp.int32, (1, LANE), 1)
    x1_head = ((lane & 63) >= MLA_NOPE) & ((lane & 63) < MLA_NOPE + MLA_ROPE // 2)
    x1_zero = lane < MLA_ROPE // 2
    cs = cs_ref[...]
    sn = sn_ref[...]

    mqa = seg('mqa', 256)
    msa = jnp.sum(mqa * mqa, axis=-1, keepdims=True) * (1.0 / MLA_QL)
    qa = (mqa * lax.rsqrt(msa + EPS) * gqa_ref[...]).astype(BF16)
    q = _dot(qa, wuq_ref[...])
    q = jnp.concatenate([_rope_lanes(q[:, :LANE], cs, sn, x1_head),
                         _rope_lanes(q[:, LANE:], cs, sn, x1_head)], axis=1)
    msq2 = _dot2(q * q, ones64_ref[...]) * (1.0 / MLA_DQK)
    qm = q * lax.rsqrt(msq2 + EPS) * gq48_ref[...]

    mkva = seg('mkva', 128)
    msc = jnp.mean(mkva * mkva, axis=-1, keepdims=True)
    c = mkva * lax.rsqrt(msc + EPS) * gkva_ref[...]
    c_ref[...] = c

    kr0 = _rope_lanes(seg('kr0', 128), cs0_ref[...], sn0_ref[...], x1_zero)
    kr16_ref[...] = kr0[:, :MLA_ROPE]

    if sample:
        kr128_ref[...] = kr0
        qg = qm * gk48_ref[...]
        qabs_ref[...] = _dot2(qg, wabs_ref[...])
        qr_ref[...] = _dot3(qg, selr_ref[...])
    else:
        qm_ref[...] = qm
        krt = seg('krt', 256)
        krt = jnp.concatenate([_rope_lanes(krt[:, :LANE], cs, sn, x1_head),
                               _rope_lanes(krt[:, LANE:], cs, sn, x1_head)], axis=1)
        kpre = _dot(c.astype(BF16), wukp_ref[...]) + krt
        msk2 = _dot2(kpre * kpre, ones64_ref[...]) * (1.0 / MLA_DQK)
        kmla_ref[...] = kpre * lax.rsqrt(msk2 + EPS) * gk48_ref[...]

    su_ref[...] = seg('su', 256)
    hq_ref[...] = seg('hq', 256)
    hf_ref[...] = seg('hf', 256)
    hi_ref[...] = seg('hi', 256)


def _inproj(x, lw, tabs, tm, n_tab, sample):
    t = x.shape[0]
    assert t % tm == 0
    const = lambda shape: pl.BlockSpec(shape, lambda i: (0,) * len(shape))
    row = lambda w: pl.BlockSpec((tm, w), lambda i: (i, 0))
    tab = pl.BlockSpec((tm, LANE), lambda i: (i % n_tab, 0))
    ins = [x, lw['g'], lw['wa'], lw['gq'], lw['gk'], lw['gqa'], lw['wuq'], lw['gq48'], lw['gkva'], lw['wukp'],
           lw['gk48'], tabs['cs'], tabs['sn'], tabs['cs0'], tabs['sn0'], lw['ones32'], lw['ones64']]
    in_specs = [row(D_MODEL), const((1, D_MODEL)), const((D_MODEL, NA)), const((1, 256)), const((1, 128)),
                const((1, 256)), const((256, 256)), const((1, 256)), const((1, 128)), const((128, 256)),
                const((1, 256)), tab, tab, tab, tab, const((256, 256)), const((256, 256))]
    if sample:
        ins += [lw['wabs'], lw['selr']]
        in_specs += [const((256, 512)), const((256, 512))]
        widths = [256, 128, 128, 128, MLA_ROPE, 128, 512, 512, 256, 256, 256, 256]
    else:
        widths = [256, 128, 128, 256, 128, 256, MLA_ROPE, 256, 256, 256, 256]
    return pl.pallas_call(
        functools.partial(_inproj_kernel, sample),
        grid=(t // tm,),
        in_specs=in_specs,
        out_specs=[row(w) for w in widths],
        out_shape=[jax.ShapeDtypeStruct((t, w), F32) for w in widths],
        compiler_params=pltpu.CompilerParams(dimension_semantics=("arbitrary",), vmem_limit_bytes=VMEM_LIMIT),
        name="inproj_sample" if sample else "inproj_prompt",
    )(*ins)


def _merge_kernel(x_ref, g_ref, wb_ref, b0_ref, b1_ref, b2_ref, b3_ref, wbr_ref, wout_ref, y_ref):
    x = x_ref[...]
    ms = jnp.mean(x * x, axis=-1, keepdims=True)
    xn = (x * lax.rsqrt(ms + EPS) * g_ref[...]).astype(BF16)
    m = jnp.zeros(x.shape, F32)
    for k, b_ref in enumerate((b0_ref, b1_ref, b2_ref, b3_ref)):
        gates = _dot(xn, wb_ref[:, k * BRANCH_W:(k + 1) * BRANCH_W])
        br = b_ref[...] * (gates * _sigmoid(gates))
        up = _dot(br.astype(BF16), wbr_ref[k])
        mg = _dot(xn, wb_ref[:, N_BRANCH * BRANCH_W + k * D_MODEL:N_BRANCH * BRANCH_W + (k + 1) * D_MODEL])
        m = m + _sigmoid(mg) * up
    y_ref[...] = x + _dot(m.astype(BF16), wout_ref[...])


def _merge(x, lw, branches, tm):
    t = x.shape[0]
    const = lambda shape: pl.BlockSpec(shape, lambda i: (0,) * len(shape))
    row = lambda w: pl.BlockSpec((tm, w), lambda i: (i, 0))
    nb = N_BRANCH * BRANCH_W + N_BRANCH * D_MODEL
    return pl.pallas_call(
        _merge_kernel,
        grid=(t // tm,),
        in_specs=[row(D_MODEL), const((1, D_MODEL)), const((D_MODEL, nb)), row(256), row(256), row(256), row(256),
                  const((N_BRANCH, BRANCH_W, D_MODEL)), const((D_MODEL, D_MODEL))],
        out_specs=row(D_MODEL),
        out_shape=jax.ShapeDtypeStruct((t, D_MODEL), F32),
        compiler_params=pltpu.CompilerParams(dimension_semantics=("arbitrary",), vmem_limit_bytes=VMEM_LIMIT),
        name="merge",
    )(x, lw['g'], lw['wb'], *branches, lw['wbr'], lw['wout'])


def _diff_slope(g, kvh):
    head = kvh * (DIFF_HEADS // DIFF_KVH) + g
    return 2.0 ** (-8.0 * (head + 1) / DIFF_HEADS)


LOG2E = 1.4426950408889634


def _diff_rows():
    return [(kvh, g, mp) for kvh in range(2) for g in range(2) for mp in range(2)]


def _fill_diff_q(qbig_ref, q, rows):
    lane = lax.broadcasted_iota(jnp.int32, (1, LANE), 1)
    for r, (kvh, g, mp) in enumerate(_diff_rows()):
        lo = kvh * 64 + mp * 32
        qbig_ref[r * rows:(r + 1) * rows, :] = jnp.where(
            (lane >= lo) & (lane < lo + DIFF_DH), q[:, g * LANE:(g + 1) * LANE], 0.0).astype(BF16)


def _diff_finish(o_of, lam, gsub, ones64, lam_init):
    lane = lax.broadcasted_iota(jnp.int32, (1, LANE), 1)
    outs = []
    for g in range(2):
        per_kvh = [o_of(kvh * 4 + g * 2) - lam * o_of(kvh * 4 + g * 2 + 1) for kvh in range(2)]
        og = jnp.where(lane < DIFF_DV, per_kvh[0], per_kvh[1])
        msq = _dot2(og * og, ones64) * (1.0 / DIFF_DV)
        outs.append(og * lax.rsqrt(msq + EPS) * gsub * (1.0 - lam_init))
    return outs


def _online_update(t, m_ref, r0, rows):
    m_old = m_ref[r0:r0 + rows, :]
    m_new = jnp.maximum(m_old, jnp.max(t, axis=-1, keepdims=True))
    m_ref[r0:r0 + rows, :] = m_new
    alpha = jnp.exp2(m_old - m_new)
    ps = [jnp.exp2(t[:, c * LANE:(c + 1) * LANE] - m_new) for c in range(t.shape[1] // LANE)]
    return alpha, ps


def _diffp_kernel(tq, tk, lam_init, lam_ref, q_ref, k_ref, v_ref, gsub_ref, ones64_ref, o_ref,
                  qbig_ref, kb_ref, vb_ref, p_ref, m_ref, acc_ref):
    qi = pl.program_id(1)
    lane = lax.broadcasted_iota(jnp.int32, (1, LANE), 1)

    @pl.when(qi == 0)
    def _():
        kb_ref[...] = k_ref[0].astype(BF16)
        v = v_ref[0]
        vb_ref[0] = jnp.where(lane < DIFF_DV, v, 1.0).astype(BF16)
        vb_ref[1] = jnp.where(lane < DIFF_DV, 1.0, v).astype(BF16)

    _fill_diff_q(qbig_ref, q_ref[0], tq)
    m_ref[...] = jnp.full(m_ref.shape, NEG, F32)
    acc_ref[...] = jnp.zeros(acc_ref.shape, F32)
    c1 = DIFF_DH ** -0.5 * LOG2E
    qpos = qi * tq + lax.broadcasted_iota(jnp.int32, (tq, 1), 0)

    def kv_step(j, masked):
        s_all = _dot_nt(qbig_ref[...], kb_ref[pl.ds(j * tk, tk), :])
        kpos = j * tk + lax.broadcasted_iota(jnp.int32, (1, tk), 1)
        kposf = kpos.astype(F32)
        for r, (kvh, g, mp) in enumerate(_diff_rows()):
            t = s_all[r * tq:(r + 1) * tq, :] * c1 + (_diff_slope(g, kvh) * LOG2E) * kposf
            if masked:
                t = jnp.where(kpos <= qpos, t, NEG)
            alpha, ps = _online_update(t, m_ref, r * tq, tq)
            acc_ref[r * tq:(r + 1) * tq, :] = alpha * acc_ref[r * tq:(r + 1) * tq, :]
            for c, p in enumerate(ps):
                p_ref[r * tq:(r + 1) * tq, c * LANE:(c + 1) * LANE] = p.astype(BF16)
        for kvh in range(2):
            rs = slice(kvh * 4 * tq, (kvh + 1) * 4 * tq)
            acc_ref[rs, :] = acc_ref[rs, :] + _dot(p_ref[rs, :], vb_ref[kvh, pl.ds(j * tk, tk), :])

    n_full = (qi * tq) // tk
    lax.fori_loop(0, n_full, lambda j, c: (kv_step(j, False), c)[1], 0)
    kv_step(n_full, True)

    def o_of(r):
        a = acc_ref[r * tq:(r + 1) * tq, :]
        return a / pltpu.roll(a, DIFF_DV, 1)

    outs = _diff_finish(o_of, lam_ref[0], gsub_ref[...], ones64_ref[...], lam_init)
    o_ref[0, :, 0:LANE] = outs[0]
    o_ref[0, :, LANE:2 * LANE] = outs[1]


def _diff_prompt(qd, kd, vd, lam, lw, lam_init, b, l):
    tq = min(256, l)
    tk = min(256, l)
    assert tk % tq == 0 and l % tk == 0
    kern = functools.partial(_diffp_kernel, tq, tk, lam_init)
    return pl.pallas_call(
        kern,
        grid=(b, l // tq),
        in_specs=[pl.BlockSpec(memory_space=pltpu.SMEM),
                  pl.BlockSpec((1, tq, 256), lambda bi, qi: (bi, qi, 0)),
                  pl.BlockSpec((1, l, 128), lambda bi, qi: (bi, 0, 0)),
                  pl.BlockSpec((1, l, 128), lambda bi, qi: (bi, 0, 0)),
                  pl.BlockSpec((1, 128), lambda bi, qi: (0, 0)),
                  pl.BlockSpec((128, 128), lambda bi, qi: (0, 0))],
        out_specs=pl.BlockSpec((1, tq, 256), lambda bi, qi: (bi, qi, 0)),
        out_shape=jax.ShapeDtypeStruct((b, l, 256), F32),
        scratch_shapes=[pltpu.VMEM((8 * tq, LANE), BF16), pltpu.VMEM((l, LANE), BF16),
                        pltpu.VMEM((2, l, LANE), BF16), pltpu.VMEM((8 * tq, tk), BF16),
                        pltpu.VMEM((8 * tq, LANE), F32), pltpu.VMEM((8 * tq, LANE), F32)],
        compiler_params=pltpu.CompilerParams(dimension_semantics=("arbitrary", "arbitrary"),
                                             vmem_limit_bytes=VMEM_LIMIT),
        name="diff_prompt",
    )(lam, qd.reshape(b, l, 256), kd.reshape(b, l, 128), vd.reshape(b, l, 128), lw['gsub'], lw['ones64_128'])


def _fill_mla_q(qbig_ref, q, rows):
    lane = lax.broadcasted_iota(jnp.int32, (1, 2 * LANE), 1)
    for h in range(MLA_HEADS):
        qbig_ref[h * rows:(h + 1) * rows, :] = jnp.where(
            (lane >= h * 64) & (lane < (h + 1) * 64), q, 0.0).astype(BF16)


def _mla_finish(acc_ref, l_ref, wuv_ref, rows):
    o = jnp.zeros((rows, 2 * LANE), F32)
    for h in range(MLA_HEADS):
        den = jnp.sum(l_ref[h * rows:(h + 1) * rows, :], axis=-1, keepdims=True)
        olat = acc_ref[h * rows:(h + 1) * rows, :] / den
        o = o + _dot(olat.astype(BF16), wuv_ref[h])
    return o


def _mlap_kernel(tq, tk, q_ref, k_ref, c_ref, wuv_ref, o_ref, qbig_ref, kb_ref, cb_ref, p_ref, m_ref, l_ref,
                 acc_ref):
    qi = pl.program_id(1)

    @pl.when(qi == 0)
    def _():
        kb_ref[...] = k_ref[0].astype(BF16)
        cb_ref[...] = c_ref[0].astype(BF16)

    _fill_mla_q(qbig_ref, q_ref[0], tq)
    m_ref[...] = jnp.full(m_ref.shape, NEG, F32)
    l_ref[...] = jnp.zeros(l_ref.shape, F32)
    acc_ref[...] = jnp.zeros(acc_ref.shape, F32)
    c1 = MLA_DQK ** -0.5 * LOG2E
    qpos = qi * tq + lax.broadcasted_iota(jnp.int32, (tq, 1), 0)

    def kv_step(j, masked):
        s_all = _dot_nt(qbig_ref[...], kb_ref[pl.ds(j * tk, tk), :])
        kpos = j * tk + lax.broadcasted_iota(jnp.int32, (1, tk), 1)
        for h in range(MLA_HEADS):
            rs = slice(h * tq, (h + 1) * tq)
            t = s_all[rs, :] * c1
            if masked:
                t = jnp.where(kpos <= qpos, t, NEG)
            alpha, ps = _online_update(t, m_ref, h * tq, tq)
            acc_ref[rs, :] = alpha * acc_ref[rs, :]
            lsum = alpha * l_ref[rs, :]
            for c, p in enumerate(ps):
                lsum = lsum + p
                p_ref[rs, c * LANE:(c + 1) * LANE] = p.astype(BF16)
            l_ref[rs, :] = lsum
        acc_ref[...] = acc_ref[...] + _dot(p_ref[...], cb_ref[pl.ds(j * tk, tk), :])

    n_full = (qi * tq) // tk
    lax.fori_loop(0, n_full, lambda j, c: (kv_step(j, False), c)[1], 0)
    kv_step(n_full, True)
    o_ref[0] = _mla_finish(acc_ref, l_ref, wuv_ref, tq)


def _mla_prompt(qm, kmla, c, lw, b, l):
    tq = min(256, l)
    tk = min(256, l)
    assert tk % tq == 0 and l % tk == 0
    nr = MLA_HEADS * tq
    return pl.pallas_call(
        functools.partial(_mlap_kernel, tq, tk),
        grid=(b, l // tq),
        in_specs=[pl.BlockSpec((1, tq, 256), lambda bi, qi: (bi, qi, 0)),
                  pl.BlockSpec((1, l, 256), lambda bi, qi: (bi, 0, 0)),
                  pl.BlockSpec((1, l, 128), lambda bi, qi: (bi, 0, 0)),
                  pl.BlockSpec((MLA_HEADS, 128, 256), lambda bi, qi: (0, 0, 0))],
        out_specs=pl.BlockSpec((1, tq, 256), lambda bi, qi: (bi, qi, 0)),
        out_shape=jax.ShapeDtypeStruct((b, l, 256), F32),
        scratch_shapes=[pltpu.VMEM((nr, 2 * LANE), BF16), pltpu.VMEM((l, 2 * LANE), BF16),
                        pltpu.VMEM((l, LANE), BF16), pltpu.VMEM((nr, tk), BF16),
                        pltpu.VMEM((nr, LANE), F32), pltpu.VMEM((nr, LANE), F32), pltpu.VMEM((nr, LANE), F32)],
        compiler_params=pltpu.CompilerParams(dimension_semantics=("arbitrary", "arbitrary"),
                                             vmem_limit_bytes=VMEM_LIMIT),
        name="mla_prompt",
    )(qm.reshape(b, l, 256), kmla.reshape(b, l, 256), c.reshape(b, l, 128), lw['wuv'])


def _sample_attn_kernel(ppc, nch, n_pages, ns, past_len, lam_init, layer, pt_ref, lam_ref,
                        qd_ref, qabs_ref, qr_ref, kn_ref, vn_ref, cn_ref, krn_ref,
                        wukt_ref, ones8_ref, wuv_ref, gsub_ref, ones64_ref,
                        ckt_hbm, cvt_hbm, cc_hbm, ckrt_hbm, od_ref, om_ref,
                        kbuf, vbuf, cbuf, krbuf, sems,
                        qd_s, qw_s, qr_s, m1, l1, acc1, m2, l2, acc2, pad_s):
    b = pl.program_id(0)
    last_chunk = pl.num_programs(0) * nch - 1
    nd = 8 * ns
    nm = MLA_HEADS * ns
    c1d = DIFF_DH ** -0.5 * LOG2E
    c1m = MLA_DQK ** -0.5 * LOG2E

    def page_copies(chunk, slot):
        base = chunk * ppc
        copies = []
        for j in range(ppc):
            pg = pt_ref[base + j]
            tok = pl.ds(j * PAGE, PAGE)
            for a, (src, dst) in enumerate(((ckt_hbm, kbuf.at[slot, :, tok]), (cvt_hbm, vbuf.at[slot, :, tok]),
                                            (cc_hbm, cbuf.at[slot, tok, :]), (ckrt_hbm, krbuf.at[slot, :, tok]))):
                copies.append(pltpu.make_async_copy(src.at[layer, pg], dst, sems.at[slot, a, j]))
        return copies

    def start_copies(chunk, slot):
        for k, cp in enumerate(page_copies(chunk, slot)):
            cp.start(priority=(k // 4) % 2)

    @pl.when(b == 0)
    def _():
        start_copies(0, 0)

    _fill_diff_q(qd_s, qd_ref[...], ns)
    for h in range(MLA_HEADS):
        qw_s[h * ns:(h + 1) * ns, :] = qabs_ref[:, h * LANE:(h + 1) * LANE].astype(BF16)
        qr_s[h * ns:(h + 1) * ns, :] = qr_ref[:, h * LANE:(h + 1) * LANE].astype(BF16)
    qw_s[nm:nm + LANE, :] = wukt_ref[...]
    m1[...] = jnp.full(m1.shape, NEG, F32)
    l1[...] = jnp.zeros(l1.shape, F32)
    acc1[...] = jnp.zeros(acc1.shape, F32)
    m2[...] = jnp.full(m2.shape, NEG, F32)
    l2[...] = jnp.zeros(l2.shape, F32)
    acc2[...] = jnp.zeros(acc2.shape, F32)

    rowd = lax.broadcasted_iota(jnp.int32, (nd, 1), 0)
    slope2 = jnp.zeros((nd, 1), F32)
    for r, (kvh, g, mp) in enumerate(_diff_rows()):
        slope2 = jnp.where((rowd >= r * ns) & (rowd < (r + 1) * ns), _diff_slope(g, kvh) * LOG2E, slope2)
    qrow_d = rowd & (ns - 1)
    qrow_m = lax.broadcasted_iota(jnp.int32, (nm, 1), 0) & (ns - 1)

    def mla_scores(r, rope_part, krsq):
        num = r[0:nm, :] + rope_part
        kt2 = r[nm:nm + LANE, :]
        sq = kt2 * kt2
        rows = []
        for h in range(MLA_HEADS):
            ssq = jnp.sum(sq[h * MLA_NOPE:(h + 1) * MLA_NOPE, :], axis=0, keepdims=True) + krsq
            inv = lax.rsqrt(ssq * (1.0 / MLA_DQK) + EPS) * c1m
            rows.append(num[h * ns:(h + 1) * ns, :] * inv)
        return jnp.concatenate(rows, axis=0)

    def accumulate(t, m_ref, l_ref, acc_ref, pv):
        alpha, ps = _online_update(t, m_ref, 0, t.shape[0])
        lsum = alpha * l_ref[...]
        for p in ps:
            lsum = lsum + p
        l_ref[...] = lsum
        p_all = ps[0] if len(ps) == 1 else jnp.concatenate(ps, axis=1)
        acc_ref[...] = alpha * acc_ref[...] + pv(p_all.astype(BF16))

    def chunk_step(ch, slot):
        chunk = b * nch + ch
        start_copies(jnp.minimum(chunk + 1, last_chunk), 1 - slot)
        for cp in page_copies(chunk, slot):
            cp.wait()
        n = ppc * PAGE
        kposf = (ch * n + lax.broadcasted_iota(jnp.int32, (1, n), 1)).astype(F32)
        t_d = _dot(qd_s[...], kbuf[slot].astype(BF16)) * c1d + slope2 * kposf
        accumulate(t_d, m1, l1, acc1, lambda p: _dot_nt(p, vbuf[slot].astype(BF16)))
        cb = cbuf[slot].astype(BF16)
        krt = krbuf[slot]
        rope_part = _dot(qr_s[:, 0:MLA_ROPE], krt.astype(BF16))
        t_m = mla_scores(_dot_nt(qw_s[...], cb), rope_part, jnp.sum(krt * krt, axis=0, keepdims=True))
        accumulate(t_m, m2, l2, acc2, lambda p: _dot(p, cb))

    def chunk_pair(k, carry):
        chunk_step(2 * k, 0)
        chunk_step(2 * k + 1, 1)
        return carry

    lax.fori_loop(0, nch // 2, chunk_pair, 0)

    jpos = lax.broadcasted_iota(jnp.int32, (1, PAGE), 1)
    pad_s[...] = jnp.zeros(pad_s.shape, F32)
    pad_s[0, 0:ns, :] = kn_ref[...]
    pad_s[1, 0:ns, :] = vn_ref[...]
    pad_s[2, 0:ns, :] = cn_ref[...]
    pad_s[3, 0:ns, :] = krn_ref[...]
    knb = pad_s[0].astype(BF16)
    vnb = pad_s[1].astype(BF16)
    cnb = pad_s[2].astype(BF16)
    krn = pad_s[3]
    t = _dot_nt(qd_s[...], knb) * c1d + slope2 * (past_len + jpos).astype(F32)
    accumulate(jnp.where(jpos <= qrow_d, t, NEG), m1, l1, acc1, lambda p: _dot(p, vnb))
    hi, lo = _split2(krn * krn)
    krsq = (_dot_nt(ones8_ref[...], hi) + _dot_nt(ones8_ref[...], lo))[0:1, :]
    t = mla_scores(_dot_nt(qw_s[...], cnb), _dot_nt(qr_s[...], krn.astype(BF16)), krsq)
    accumulate(jnp.where(jpos <= qrow_m, t, NEG), m2, l2, acc2, lambda p: _dot(p, cnb))

    def o_of(r):
        den = jnp.sum(l1[r * ns:(r + 1) * ns, :], axis=-1, keepdims=True)
        return acc1[r * ns:(r + 1) * ns, :] / den

    outs = _diff_finish(o_of, lam_ref[0], gsub_ref[...], ones64_ref[...], lam_init)
    od_ref[:, 0:LANE] = outs[0]
    od_ref[:, LANE:2 * LANE] = outs[1]
    om_ref[...] = _mla_finish(acc2, l2, wuv_ref, ns)

    @pl.when(b == pl.num_programs(0) - 1)
    def _():
        for cp in page_copies(last_chunk, 0):
            cp.wait()


def _sample_attn(layer, lam, lw, lam_init, page_table, caches, qd, qabs, qr, kn, vn, cn, krn, nb, ns):
    n_pages = page_table.shape[1]
    ppc = math.gcd(n_pages, 32)
    nch = n_pages // ppc
    assert ns & (ns - 1) == 0 and nch % 2 == 0
    past_len = n_pages * PAGE
    pt_flat = page_table.reshape(-1)
    seq = lambda w: pl.BlockSpec((ns, w), lambda i, pt: (i, 0))
    const = lambda shape: pl.BlockSpec(shape, lambda i, pt: (0,) * len(shape))
    hbm = pl.BlockSpec(memory_space=pl.ANY)
    in_specs = [pl.BlockSpec(memory_space=pltpu.SMEM),
                seq(256), seq(512), seq(512), seq(128), seq(128), seq(128), seq(128),
                const((128, 128)), const((8, 128)), const((MLA_HEADS, 128, 256)),
                const((1, 128)), const((128, 128)), hbm, hbm, hbm, hbm]
    ins = [lam, qd, qabs, qr, kn, vn, cn, krn, lw['wukt'], lw['ones8'], lw['wuv'], lw['gsub'],
           lw['ones64_128'], *caches]
    nd, nm = 8 * ns, MLA_HEADS * ns
    kern = functools.partial(_sample_attn_kernel, ppc, nch, n_pages, ns, past_len, lam_init, layer)
    n = ppc * PAGE
    wide_buf = lambda rows: pltpu.VMEM((2, rows, n), F32)
    return pl.pallas_call(
        kern,
        grid_spec=pltpu.PrefetchScalarGridSpec(
            num_scalar_prefetch=1,
            grid=(nb,),
            in_specs=in_specs,
            out_specs=[seq(256), seq(256)],
            scratch_shapes=[wide_buf(PAGE), wide_buf(PAGE), pltpu.VMEM((2, n, LANE), F32), wide_buf(MLA_ROPE),
                            pltpu.SemaphoreType.DMA((2, 4, ppc)),
                            pltpu.VMEM((nd, LANE), BF16), pltpu.VMEM((nm + LANE, LANE), BF16),
                            pltpu.VMEM((nm, LANE), BF16),
                            pltpu.VMEM((nd, LANE), F32), pltpu.VMEM((nd, LANE), F32), pltpu.VMEM((nd, LANE), F32),
                            pltpu.VMEM((nm, LANE), F32), pltpu.VMEM((nm, LANE), F32), pltpu.VMEM((nm, LANE), F32),
                            pltpu.VMEM((4, PAGE, LANE), F32)]),
        out_shape=[jax.ShapeDtypeStruct((nb * ns, 256), F32), jax.ShapeDtypeStruct((nb * ns, 256), F32)],
        compiler_params=pltpu.CompilerParams(dimension_semantics=("arbitrary",), vmem_limit_bytes=VMEM_LIMIT),
        name="sample_attn",
    )(pt_flat, *ins)


N_ST = 2 * SSM_W // LANE


def _s5_pitch(tb):
    return tb + SUB if (tb // SUB) % 2 == 0 else tb


def _s5_kernel(tb, u_ref, h0re_ref, h0im_ref, bbar_ref, are_ref, aim_ref, cmat_ref, d_ref, wglu_ref,
               o_ref, hre_ref, him_ref, ubt, utm, bus, hs, otm, obt, hst):
    tblk = pl.program_id(1)
    pitch = _s5_pitch(tb)
    half = N_ST // 2
    nlt = BRANCH_W // LANE

    @pl.when(tblk == 0)
    def _():
        for j in range(half):
            hst[j] = h0re_ref[:, j * LANE:(j + 1) * LANE]
            hst[half + j] = h0im_ref[:, j * LANE:(j + 1) * LANE]

    for bi in range(SUB):
        for c in range(nlt):
            ubt[c, bi * pitch:bi * pitch + tb, :] = u_ref[bi, :, c * LANE:(c + 1) * LANE]

    def to_time_major(t, carry):
        r0 = pl.multiple_of(t * SUB, SUB)
        for c in range(nlt):
            utm[pl.ds(r0, SUB), c * LANE:(c + 1) * LANE] = ubt[c, pl.ds(t, SUB, stride=pitch), :]
        return carry

    lax.fori_loop(0, tb, to_time_major, 0)
    bus[...] = _dot(utm[...].astype(BF16), bbar_ref[...])
    ar = [jnp.broadcast_to(are_ref[:, j * LANE:(j + 1) * LANE], (SUB, LANE)) for j in range(half)]
    ai = [jnp.broadcast_to(aim_ref[:, j * LANE:(j + 1) * LANE], (SUB, LANE)) for j in range(half)]

    def step(t, st):
        r0 = pl.multiple_of(t * SUB, SUB)
        new = [None] * N_ST
        for j in range(half):
            hr, hi = st[j], st[half + j]
            nr = ar[j] * hr - ai[j] * hi + bus[pl.ds(r0, SUB), j * LANE:(j + 1) * LANE]
            ni = ar[j] * hi + ai[j] * hr + bus[pl.ds(r0, SUB), (half + j) * LANE:(half + j + 1) * LANE]
            hs[pl.ds(r0, SUB), j * LANE:(j + 1) * LANE] = nr
            hs[pl.ds(r0, SUB), (half + j) * LANE:(half + j + 1) * LANE] = ni
            new[j], new[half + j] = nr, ni
        return tuple(new)

    st = lax.fori_loop(0, tb, step, tuple(hst[j] for j in range(N_ST)))
    for j in range(N_ST):
        hst[j] = st[j]

    y = d_ref[...] * utm[...] + _dot(hs[...].astype(BF16), cmat_ref[...])
    yg = _dot(y.astype(BF16), wglu_ref[...])
    otm[...] = yg[:, :BRANCH_W] * _sigmoid(yg[:, BRANCH_W:])

    def to_batch_major(t, carry):
        r0 = pl.multiple_of(t * SUB, SUB)
        for c in range(nlt):
            obt[c, pl.ds(t, SUB, stride=pitch), :] = otm[pl.ds(r0, SUB), c * LANE:(c + 1) * LANE]
        return carry

    lax.fori_loop(0, tb, to_batch_major, 0)
    for bi in range(SUB):
        for c in range(nlt):
            o_ref[bi, :, c * LANE:(c + 1) * LANE] = obt[c, bi * pitch:bi * pitch + tb, :]

    @pl.when(tblk == pl.num_programs(1) - 1)
    def _():
        for j in range(half):
            hre_ref[:, j * LANE:(j + 1) * LANE] = st[j]
            him_ref[:, j * LANE:(j + 1) * LANE] = st[half + j]


def _s5(su, h0re, h0im, lw, b, l):
    assert b % SUB == 0
    tb = min(64, l)
    rows = SUB * tb
    const = lambda shape: pl.BlockSpec(shape, lambda bi, ti: (0,) * len(shape))
    st_spec = pl.BlockSpec((SUB, SSM_W), lambda bi, ti: (bi, 0))
    return pl.pallas_call(
        functools.partial(_s5_kernel, tb),
        grid=(b // SUB, l // tb),
        in_specs=[pl.BlockSpec((SUB, tb, BRANCH_W), lambda bi, ti: (bi, ti, 0)), st_spec, st_spec,
                  const((BRANCH_W, 2 * SSM_W)), const((1, SSM_W)), const((1, SSM_W)),
                  const((2 * SSM_W, BRANCH_W)), const((1, BRANCH_W)), const((BRANCH_W, 2 * BRANCH_W))],
        out_specs=[pl.BlockSpec((SUB, tb, BRANCH_W), lambda bi, ti: (bi, ti, 0)), st_spec, st_spec],
        out_shape=[jax.ShapeDtypeStruct((b, l, BRANCH_W), F32), jax.ShapeDtypeStruct((b, SSM_W), F32),
                   jax.ShapeDtypeStruct((b, SSM_W), F32)],
        scratch_shapes=[pltpu.VMEM((BRANCH_W // LANE, SUB * _s5_pitch(tb), LANE), F32),
                        pltpu.VMEM((rows, BRANCH_W), F32),
                        pltpu.VMEM((rows, 2 * SSM_W), F32), pltpu.VMEM((rows, 2 * SSM_W), F32),
                        pltpu.VMEM((rows, BRANCH_W), F32),
                        pltpu.VMEM((BRANCH_W // LANE, SUB * _s5_pitch(tb), LANE), F32),
                        pltpu.VMEM((N_ST, SUB, LANE), F32)],
        compiler_params=pltpu.CompilerParams(dimension_semantics=("arbitrary", "arbitrary"),
                                             vmem_limit_bytes=VMEM_LIMIT),
        name="s5",
    )(su.reshape(b, l, BRANCH_W), h0re, h0im, lw['bbar'], lw['a_re'], lw['a_im'], lw['cmat'], lw['ssm_d'], lw['wglu'])


def _hgrn_kernel(chunk, nchunk, chain, q_ref, f_ref, v_ref, s0_ref, lb_ref, gain_ref, tri_ref, same_ref,
                 ones64_ref, o_ref, sfin_ref, kp, bcp, vp, st, oacc):
    rows = chunk * nchunk
    q = q_ref[0]
    zf = f_ref[0]
    v = v_ref[0]
    lb = lb_ref[...]
    logf = jnp.log(lb + (1.0 - lb) * _sigmoid(zf))
    kk = (1.0 - lb) * _sigmoid(-zf)
    bc = _dot3t(tri_ref[...], logf)
    bl = _dot3t(same_ref[...], logf)
    qh = q * jnp.exp(bc)
    kdec = (kk * jnp.exp(bl - bc)).astype(BF16)

    kp[0:HG_CHUNK, :] = jnp.zeros((HG_CHUNK, BRANCH_W), F32)
    bcp[0:HG_CHUNK, :] = jnp.zeros((HG_CHUNK, BRANCH_W), F32)
    vp[0:HG_CHUNK, :] = jnp.zeros((HG_CHUNK, BRANCH_W), F32)
    kp[HG_CHUNK:HG_CHUNK + rows, :] = kk
    bcp[HG_CHUNK:HG_CHUNK + rows, :] = bc
    vp[HG_CHUNK:HG_CHUNK + rows, :] = v
    rpos = lax.broadcasted_iota(jnp.int32, (rows, 1), 0) & (chunk - 1)
    o = _dot((q * kk).astype(BF16), ones64_ref[...]) * v
    for d in range(1, chunk):
        lo = HG_CHUNK - d
        ks = kp[lo:lo + rows, :]
        bcs = bcp[lo:lo + rows, :]
        vs = vp[lo:lo + rows, :]
        x = q * ks * jnp.exp(jnp.where(rpos >= d, bc - bcs, NEG))
        o = o + _dot(x.astype(BF16), ones64_ref[...]) * vs
    oacc[...] = o

    lane = lax.broadcasted_iota(jnp.int32, (1, BRANCH_W), 1)
    rowi = lax.broadcasted_iota(jnp.int32, (rows, 1), 0)
    head_lanes = [(lane >= h * HG_DK) & (lane < (h + 1) * HG_DK) for h in range(HG_HEADS)]
    kvs = []
    for i in range(nchunk):
        r0 = i * chunk
        vmask = jnp.where((rowi >= r0) & (rowi < r0 + chunk), v, 0.0).astype(BF16)
        kvt = _dot_tn(vmask, kdec)
        comp = jnp.zeros((HG_DV, BRANCH_W), F32)
        for h in range(HG_HEADS):
            comp = jnp.where(head_lanes[h], kvt[h * HG_DV:(h + 1) * HG_DV, :], comp)
        kvs.append(comp)
    if chain:
        @pl.when(pl.program_id(1) == 0)
        def _():
            st[...] = s0_ref[0]
        s_t = st[...]
    for i in range(nchunk):
        r0 = i * chunk
        if not chain:
            s_t = s0_ref[i]
        full = jnp.concatenate([jnp.where(head_lanes[h], s_t, 0.0) for h in range(HG_HEADS)],
                               axis=0).astype(BF16)
        oacc[r0:r0 + chunk, :] = oacc[r0:r0 + chunk, :] + _dot_nt(qh[r0:r0 + chunk, :].astype(BF16), full)
        s_t = s_t * jnp.exp(bl[r0:r0 + 1, :]) + kvs[i]
        if not chain:
            sfin_ref[i] = s_t
    if chain:
        st[...] = s_t
        sfin_ref[0] = s_t
    ot = oacc[...]
    msq = _dot2(ot * ot, ones64_ref[...]) * (1.0 / HG_DV)
    o_ref[0] = ot * lax.rsqrt(msq + EPS) * gain_ref[...]


def _dot3t(m, x):
    hi, mid, lo = _split3(x)
    return _dot(m, hi) + _dot(m, mid) + _dot(m, lo)


def _hgrn(hq, hf, hi, s0t, lw, b, l):
    chunk = math.gcd(l, HG_CHUNK)
    if l >= PAGE:
        chain, rows, nblk, outer = True, PAGE, l // PAGE, b
    else:
        assert l == chunk
        per = min(b, PAGE // l)
        chain, rows, nblk, outer = False, per * l, 1, b // per
    nchunk = rows // chunk
    sblk = 1 if chain else nchunk
    r = np.arange(rows)
    same = (r[:, None] // chunk) == (r[None, :] // chunk)
    tri = same & (r[None, :] <= r[:, None])
    blk = lambda: pl.BlockSpec((1, rows, BRANCH_W), lambda bi, ti: (bi * nblk + ti, 0, 0))
    const = lambda shape: pl.BlockSpec(shape, lambda bi, ti: (0,) * len(shape))
    st_spec = pl.BlockSpec((sblk, HG_DV, BRANCH_W), lambda bi, ti: (bi, 0, 0))
    shp = (outer * nblk, rows, BRANCH_W)
    o, sfin = pl.pallas_call(
        functools.partial(_hgrn_kernel, chunk, nchunk, chain),
        grid=(outer, nblk),
        in_specs=[blk(), blk(), blk(), st_spec, const((1, BRANCH_W)), const((1, BRANCH_W)),
                  const((rows, rows)), const((rows, rows)), const((BRANCH_W, BRANCH_W))],
        out_specs=[blk(), st_spec],
        out_shape=[jax.ShapeDtypeStruct(shp, F32), jax.ShapeDtypeStruct((b, HG_DV, BRANCH_W), F32)],
        scratch_shapes=[pltpu.VMEM((HG_CHUNK + rows, BRANCH_W), F32), pltpu.VMEM((HG_CHUNK + rows, BRANCH_W), F32),
                        pltpu.VMEM((HG_CHUNK + rows, BRANCH_W), F32), pltpu.VMEM((HG_DV, BRANCH_W), F32),
                        pltpu.VMEM((rows, BRANCH_W), F32)],
        compiler_params=pltpu.CompilerParams(dimension_semantics=("arbitrary", "arbitrary"),
                                             vmem_limit_bytes=VMEM_LIMIT),
        name="hgrn_chain" if chain else "hgrn_step",
    )(hq.reshape(shp), hf.reshape(shp), hi.reshape(shp), s0t, lw['lb'], lw['hg_gain'],
      jnp.asarray(tri, BF16), jnp.asarray(same, BF16), lw['ones64'])
    return o.reshape(b * l, BRANCH_W), sfin


def _block_ones(n, blk):
    r = np.arange(n)
    return jnp.asarray((r[:, None] // blk) == (r[None, :] // blk), BF16)


def _layout_indices():
    zero = IN_COLS
    a = np.full((NA,), zero, np.int64)
    for g in range(2):
        for kvh in range(2):
            for mp in range(2):
                dst = _A['dq'] + g * 128 + kvh * 64 + mp * 32
                src = _OFF['dq'] + kvh * 128 + g * 64 + mp * 32
                a[dst:dst + 32] = np.arange(src, src + 32)
    a[_A['dk']:_A['dk'] + 128] = np.arange(_OFF['dk'], _OFF['dk'] + 128)
    a[_A['dv']:_A['dv'] + 128] = np.arange(_OFF['dv'], _OFF['dv'] + 128)
    a[_A['mqa']:_A['mqa'] + MLA_QL] = np.arange(_OFF['mqa'], _OFF['mqa'] + MLA_QL)
    a[_A['mkva']:_A['mkva'] + 128] = np.arange(_OFF['mkva'], _OFF['mkva'] + 128)
    for h in range(MLA_HEADS):
        dst = _A['krt'] + h * 64 + MLA_NOPE
        a[dst:dst + MLA_ROPE] = np.arange(_OFF['mkr'], _OFF['mkr'] + MLA_ROPE)
    a[_A['kr0']:_A['kr0'] + MLA_ROPE] = np.arange(_OFF['mkr'], _OFF['mkr'] + MLA_ROPE)
    for name in ('su', 'hq', 'hf', 'hi'):
        a[_A[name]:_A[name] + 256] = np.arange(_OFF[name], _OFF[name] + 256)
    perm = np.zeros((256,), np.int64)
    for g in range(2):
        for kvh in range(2):
            perm[g * 128 + kvh * 64:g * 128 + kvh * 64 + 64] = np.arange(kvh * 128 + g * 64, kvh * 128 + g * 64 + 64)
    bcols = np.concatenate([_OFF['gates'] + perm, np.arange(_OFF['gates'] + 256, _OFF['gates'] + 1024),
                            np.arange(_OFF['merge'], _OFF['merge'] + N_BRANCH * D_MODEL)])
    return a, bcols, perm


_IDX_A, _IDX_B, _PERM_DIFF = _layout_indices()


def _head48(vec):
    return jnp.tile(jnp.pad(vec, (0, 64 - MLA_DQK)), MLA_HEADS).reshape(1, 256)


def _take_runs(w, idx, axis, zero):
    idx = np.asarray(idx)
    cuts = [0] + [i for i in range(1, len(idx))
                  if not ((idx[i] == zero and idx[i - 1] == zero) or
                          (idx[i] != zero and idx[i - 1] != zero and idx[i] == idx[i - 1] + 1))] + [len(idx)]
    pieces = []
    for a, b in zip(cuts[:-1], cuts[1:]):
        if idx[a] == zero:
            shape = list(w.shape)
            shape[axis] = b - a
            pieces.append(jnp.zeros(shape, w.dtype))
        else:
            pieces.append(lax.slice_in_dim(w, int(idx[a]), int(idx[a]) + (b - a), axis=axis))
    return jnp.concatenate(pieces, axis=axis)


def _prep_layer(p, l, lb_all):
    w_in = p['w_in'][l].astype(BF16)
    lw = {}
    lw['g'] = p['norm_gain'][l].reshape(1, D_MODEL)
    lw['wa'] = _take_runs(w_in, _IDX_A, 1, IN_COLS)
    lw['wb'] = _take_runs(w_in, _IDX_B, 1, IN_COLS)
    wbr = p['w_branch'][l]
    lw['wbr'] = jnp.concatenate([_take_runs(wbr[0], _PERM_DIFF, 0, -1)[None], wbr[1:]], axis=0).astype(BF16)
    lw['wout'] = p['w_out'][l].astype(BF16)
    lw['gq'] = jnp.tile(p['diff_q_gain'][l], 8).reshape(1, 256)
    lw['gk'] = jnp.tile(p['diff_k_gain'][l], 4).reshape(1, 128)
    lw['gsub'] = jnp.tile(p['diff_subln_gain'][l], 2).reshape(1, 128)
    lw['gqa'] = jnp.pad(p['mla_qa_gain'][l], (0, 256 - MLA_QL)).reshape(1, 256)
    wuq = p['w_mla_uq'][l].reshape(MLA_QL, MLA_HEADS, MLA_DQK)
    lw['wuq'] = jnp.pad(wuq, ((0, 256 - MLA_QL), (0, 0), (0, 64 - MLA_DQK))).reshape(256, 256).astype(BF16)
    lw['gq48'] = _head48(p['mla_q_gain'][l])
    lw['gk48'] = _head48(p['mla_k_gain'][l])
    lw['gkva'] = p['mla_kva_gain'][l].reshape(1, 128)
    wuk = p['w_mla_uk'][l]
    lw['wukp'] = jnp.pad(wuk, ((0, 0), (0, 0), (0, 64 - MLA_NOPE))).reshape(128, 256).astype(BF16)
    lw['wuk2d'] = wuk.reshape(128, MLA_HEADS * MLA_NOPE).astype(BF16)
    wabs = jnp.zeros((MLA_HEADS, 64, MLA_HEADS, 128), F32)
    selr = np.zeros((MLA_HEADS, 64, MLA_HEADS, 128), np.float32)
    for h in range(MLA_HEADS):
        wabs = wabs.at[h, :MLA_NOPE, h, :].set(wuk[:, h, :].T)
        selr[h, MLA_NOPE + np.arange(MLA_ROPE), h, np.arange(MLA_ROPE)] = 1.0
    lw['wabs'] = wabs.reshape(256, 512).astype(BF16)
    lw['selr'] = jnp.asarray(selr.reshape(256, 512), BF16)
    wuv = p['w_mla_uv'][l]
    wuvp = jnp.zeros((MLA_HEADS, 128, MLA_HEADS, MLA_DV), F32)
    for h in range(MLA_HEADS):
        wuvp = wuvp.at[h, :, h, :].set(wuv[:, h, :])
    lw['wuv'] = wuvp.reshape(MLA_HEADS, 128, 256).astype(BF16)
    lw['wukt'] = wuk.reshape(128, MLA_HEADS * MLA_NOPE).T.astype(BF16)
    ones8 = np.zeros((8, 128), np.float32)
    ones8[:, :MLA_ROPE] = 1.0
    lw['ones8'] = jnp.asarray(ones8, BF16)
    lw['ones32'] = _block_ones(256, 32)
    lw['ones64'] = _block_ones(256, 64)
    lw['ones64_128'] = _block_ones(128, 64)

    lre, lim = p['ssm_a_re'][l], p['ssm_a_im'][l]
    dt = jnp.exp(p['ssm_log_dt'][l])[:, None]
    mag = jnp.exp(lre * dt)
    are, aim = mag * jnp.cos(lim * dt), mag * jnp.sin(lim * dt)
    den = lre * lre + lim * lim
    cre = ((are - 1.0) * lre + aim * lim) / den
    cim = (aim * lre - (are - 1.0) * lim) / den
    bre, bim = p['ssm_b_re'][l], p['ssm_b_im'][l]
    bbre = cre[..., None] * bre - cim[..., None] * bim
    bbim = cre[..., None] * bim + cim[..., None] * bre
    eye = jnp.eye(SSM_GROUPS, dtype=F32)
    b_re = jnp.einsum('gph,gk->ghkp', bbre, eye).reshape(BRANCH_W, SSM_W)
    b_im = jnp.einsum('gph,gk->ghkp', bbim, eye).reshape(BRANCH_W, SSM_W)
    lw['bbar'] = jnp.concatenate([b_re, b_im], axis=1).astype(BF16)
    c_re = jnp.einsum('ghp,gk->gpkh', p['ssm_c_re'][l], eye).reshape(SSM_W, BRANCH_W)
    c_im = jnp.einsum('ghp,gk->gpkh', p['ssm_c_im'][l], eye).reshape(SSM_W, BRANCH_W)
    lw['cmat'] = jnp.concatenate([c_re, -c_im], axis=0).astype(BF16)
    lw['a_re'] = are.reshape(1, SSM_W)
    lw['a_im'] = aim.reshape(1, SSM_W)
    lw['ssm_d'] = p['ssm_d'][l].reshape(1, BRANCH_W)
    lw['wglu'] = p['w_glu'][l].astype(BF16)
    lw['lb'] = lb_all[l].reshape(1, BRANCH_W)
    lw['hg_gain'] = jnp.tile(p['hg_norm_gain'][l], HG_HEADS).reshape(1, BRANCH_W)
    lp = p['diff_lambda'][l]
    lam_init = 0.8 - 0.6 * math.exp(-0.3 * l)
    lw['lam'] = (jnp.exp(jnp.sum(lp[0] * lp[1])) - jnp.exp(jnp.sum(lp[2] * lp[3])) + lam_init).reshape(1)
    return lw, lam_init


def _rope_tables(pos):
    half = MLA_ROPE // 2
    freqs = ROPE_BASE ** (-jnp.arange(half, dtype=F32) / half)
    ang = pos.astype(F32)[:, None] * freqs
    cos, sin = jnp.cos(ang), jnp.sin(ang)
    n = pos.shape[0]
    one = jnp.ones((n, MLA_NOPE), F32)
    zero = jnp.zeros((n, MLA_NOPE), F32)
    pad1 = jnp.ones((n, 64 - MLA_DQK), F32)
    pad0 = jnp.zeros((n, 64 - MLA_DQK), F32)
    cs_h = jnp.concatenate([one, cos, cos, pad1], axis=1)
    sn_h = jnp.concatenate([zero, -sin, sin, pad0], axis=1)
    cs0 = jnp.concatenate([cos, cos, jnp.ones((n, LANE - MLA_ROPE), F32)], axis=1)
    sn0 = jnp.concatenate([-sin, sin, jnp.zeros((n, LANE - MLA_ROPE), F32)], axis=1)
    return dict(cs=jnp.tile(cs_h, (1, 2)), sn=jnp.tile(sn_h, (1, 2)), cs0=cs0, sn0=sn0)


def kernel(x_prompt, x_sample, cache_diff_k, cache_diff_v, cache_mla_c, cache_mla_kr, state_ssm_re, state_ssm_im, state_hgrn, page_table, norm_gain, w_in, w_branch, w_out, diff_q_gain, diff_k_gain, diff_lambda, diff_subln_gain, mla_qa_gain, mla_kva_gain, w_mla_uq, w_mla_uk, w_mla_uv, mla_q_gain, mla_k_gain, ssm_a_re, ssm_a_im, ssm_log_dt, ssm_b_re, ssm_b_im, ssm_c_re, ssm_c_im, ssm_d, w_glu, hg_lb_logits, hg_norm_gain):
    p = dict(norm_gain=norm_gain, w_in=w_in, w_branch=w_branch, w_out=w_out, diff_q_gain=diff_q_gain,
             diff_k_gain=diff_k_gain, diff_lambda=diff_lambda, diff_subln_gain=diff_subln_gain,
             mla_qa_gain=mla_qa_gain, mla_kva_gain=mla_kva_gain, w_mla_uq=w_mla_uq, w_mla_uk=w_mla_uk,
             w_mla_uv=w_mla_uv, mla_q_gain=mla_q_gain, mla_k_gain=mla_k_gain, ssm_a_re=ssm_a_re, ssm_a_im=ssm_a_im,
             ssm_log_dt=ssm_log_dt, ssm_b_re=ssm_b_re, ssm_b_im=ssm_b_im, ssm_c_re=ssm_c_re, ssm_c_im=ssm_c_im,
             ssm_d=ssm_d, w_glu=w_glu, hg_norm_gain=hg_norm_gain)
    depth = w_in.shape[0]
    bp, lp, _ = x_prompt.shape
    bs, ls, _ = x_sample.shape
    n_pages = page_table.shape[1]
    past_len = n_pages * PAGE
    tp, ts = bp * lp, bs * ls

    sm = jax.nn.softmax(hg_lb_logits.astype(F32), axis=0)
    lb_all = jnp.cumsum(sm, axis=0) - sm[0]

    tm_p = min(512, lp)
    tm_s = min(512, ts)
    assert tm_s % ls == 0
    tabs_p = _rope_tables(jnp.arange(lp))
    tabs_s = _rope_tables(past_len + (jnp.arange(tm_s) % ls))

    n_pool = cache_diff_k.shape[1]
    ckt = jnp.transpose(cache_diff_k, (0, 1, 3, 4, 2)).reshape(depth, n_pool, 128, PAGE)
    cvt = jnp.transpose(cache_diff_v, (0, 1, 3, 4, 2)).reshape(depth, n_pool, 128, PAGE)
    ckrt = jnp.transpose(cache_mla_kr, (0, 1, 3, 2))
    caches = (ckt, cvt, cache_mla_c, ckrt)

    hp = x_prompt.reshape(tp, D_MODEL)
    hs = x_sample.reshape(ts, D_MODEL)
    zeros_p = jnp.zeros((bp, SSM_W), F32)
    zeros_st = jnp.zeros((bp, HG_DV, BRANCH_W), F32)
    outs_p, outs_s = [], []
    for l in range(depth):
        lw, lam_init = _prep_layer(p, l, lb_all)
        (qd, kd, vd, qm, c, kmla, kr16, su, hq, hf, hi) = _inproj(hp, lw, tabs_p, tm_p, lp // tm_p, False)
        o_diff = _diff_prompt(qd, kd, vd, lw['lam'], lw, lam_init, bp, lp).reshape(tp, 256)
        o_mla = _mla_prompt(qm, kmla, c, lw, bp, lp).reshape(tp, 256)
        o_ssm, hre, him = _s5(su, zeros_p, zeros_p, lw, bp, lp)
        o_hg, sfin = _hgrn(hq, hf, hi, zeros_st, lw, bp, lp)
        hp = _merge(hp, lw, (o_diff, o_mla, o_ssm.reshape(tp, 256), o_hg), tm_p)
        outs_p.append((kd, vd, c, kr16, hre, him, sfin))
        (qd, kd, vd, c, kr16, kr128, qabs, qr, su, hq, hf, hi) = _inproj(hs, lw, tabs_s, tm_s, 1, True)
        o_diff, o_mla = _sample_attn(l, lw['lam'], lw, lam_init, page_table, caches,
                                     qd, qabs, qr, kd, vd, c, kr128, bs, ls)
        o_ssm, hre, him = _s5(su, state_ssm_re[l].reshape(bs, SSM_W), state_ssm_im[l].reshape(bs, SSM_W), lw, bs, ls)
        s0t = jnp.transpose(state_hgrn[l], (0, 3, 1, 2)).reshape(bs, HG_DV, BRANCH_W)
        o_hg, sfin = _hgrn(hq, hf, hi, s0t, lw, bs, ls)
        hs = _merge(hs, lw, (o_diff, o_mla, o_ssm.reshape(ts, 256), o_hg), tm_s)
        outs_s.append((kd, vd, c, kr16, hre, him, sfin))

    def stack(group, i):
        return jnp.stack([st[i] for st in group], axis=0)

    def unstate(s, b):
        return jnp.transpose(s.reshape(depth, b, HG_DV, HG_HEADS, HG_DK), (0, 1, 3, 4, 2))

    npg = lp // PAGE
    return (hp.reshape(bp, lp, D_MODEL), hs.reshape(bs, ls, D_MODEL),
            stack(outs_p, 0).reshape(depth, bp, npg, PAGE, DIFF_KVH, 2 * DIFF_DH),
            stack(outs_p, 1).reshape(depth, bp, npg, PAGE, DIFF_KVH, DIFF_DV),
            stack(outs_p, 2).reshape(depth, bp, npg, PAGE, MLA_KVL),
            stack(outs_p, 3).reshape(depth, bp, npg, PAGE, MLA_ROPE),
            stack(outs_p, 4).reshape(depth, bp, SSM_GROUPS, SSM_STATE),
            stack(outs_p, 5).reshape(depth, bp, SSM_GROUPS, SSM_STATE),
            unstate(stack(outs_p, 6), bp),
            stack(outs_s, 0).reshape(depth, bs, ls, DIFF_KVH, 2 * DIFF_DH),
            stack(outs_s, 1).reshape(depth, bs, ls, DIFF_KVH, DIFF_DV),
            stack(outs_s, 2).reshape(depth, bs, ls, MLA_KVL),
            stack(outs_s, 3).reshape(depth, bs, ls, MLA_ROPE),
            stack(outs_s, 4).reshape(depth, bs, SSM_GROUPS, SSM_STATE),
            stack(outs_s, 5).reshape(depth, bs, SSM_GROUPS, SSM_STATE),
            unstate(stack(outs_s, 6), bs))
```

```python
import functools
import math

import numpy as np
import jax
import jax.numpy as jnp
from jax import lax
from jax.experimental import pallas as pl
from jax.experimental.pallas import tpu as pltpu

F32 = jnp.float32
BF16 = jnp.bfloat16
EPS = 1e-6
NEG = -1e30

D_MODEL = 1024
BRANCH_W = 256
N_BRANCH = 4
DIFF_HEADS, DIFF_KVH, DIFF_DH, DIFF_DV = 4, 2, 32, 64
MLA_HEADS, MLA_QL, MLA_KVL, MLA_NOPE, MLA_ROPE, MLA_DV = 4, 192, 128, 32, 16, 64
MLA_DQK = MLA_NOPE + MLA_ROPE
ROPE_BASE = 10000.0
SSM_GROUP, SSM_GROUPS, SSM_STATE = 16, 16, 64
SSM_W = SSM_GROUPS * SSM_STATE
HG_HEADS, HG_DK, HG_DV, HG_CHUNK = 4, 64, 64, 16
PAGE = 128
LANE = 128
SUB = 8
VMEM_LIMIT = 56 * 1024 * 1024
TOKEN_TILE = 512
ATTN_TQ = 256
ATTN_TK = 256
S5_TIME_BLOCK = 64
PAGES_PER_CHUNK = 64

_OFF = dict(dq=0, dk=256, dv=384, mqa=512, mkva=704, mkr=832, su=848, hq=1104, hf=1360, hi=1616,
            gates=1872, merge=2896)
IN_COLS = 6992
_A = dict(dq=0, dk=256, dv=384, mqa=512, mkva=768, krt=896, kr0=1152, su=1280, hq=1536, hf=1792, hi=2048)
NA = 2304


def _dot(a, b):
    return jnp.dot(a, b, preferred_element_type=F32)


def _dot_nt(a, b):
    return lax.dot_general(a, b, (((1,), (1,)), ((), ())), preferred_element_type=F32)


def _dot_tn(a, b):
    return lax.dot_general(a, b, (((0,), (0,)), ((), ())), preferred_element_type=F32)


def _split2(x):
    hi = x.astype(BF16)
    lo = (x - hi.astype(F32)).astype(BF16)
    return hi, lo


def _split3(x):
    hi = x.astype(BF16)
    r = x - hi.astype(F32)
    mid = r.astype(BF16)
    lo = (r - mid.astype(F32)).astype(BF16)
    return hi, mid, lo


def _dot2(x, w):
    hi, lo = _split2(x)
    return _dot(hi, w) + _dot(lo, w)


def _dot3(x, w):
    hi, mid, lo = _split3(x)
    return _dot(hi, w) + _dot(mid, w) + _dot(lo, w)


def _sigmoid(x):
    return 1.0 / (1.0 + jnp.exp(-x))


def _rope_lanes(x, cs, sn, x1mask):
    rot = jnp.where(x1mask, pltpu.roll(x, LANE - MLA_ROPE // 2, 1), pltpu.roll(x, MLA_ROPE // 2, 1))
    return x * cs + rot * sn


def _inproj_kernel(sample, *refs):
    (x_ref, g_ref, wa_ref, gq_ref, gk_ref, gqa_ref, wuq_ref, gq48_ref, gkva_ref, wukp_ref, gk48_ref,
     cs_ref, sn_ref, cs0_ref, sn0_ref, ones32_ref, ones64_ref) = refs[:17]
    if sample:
        wabs_ref, selr_ref = refs[17:19]
        (qd_ref, kd_ref, vd_ref, c_ref, kr16_ref, kr128_ref, qabs_ref, qr_ref,
         su_ref, hq_ref, hf_ref, hi_ref) = refs[19:]
    else:
        (qd_ref, kd_ref, vd_ref, qm_ref, c_ref, kmla_ref, kr16_ref,
         su_ref, hq_ref, hf_ref, hi_ref) = refs[17:]

    x = x_ref[...]
    ms = jnp.mean(x * x, axis=-1, keepdims=True)
    xn = (x * lax.rsqrt(ms + EPS) * g_ref[...]).astype(BF16)
    h = _dot(xn, wa_ref[...])

    def seg(name, width):
        return h[:, _A[name]:_A[name] + width]

    dq = seg('dq', 256)
    msq = _dot2(dq * dq, ones32_ref[...]) * (1.0 / DIFF_DH)
    qd_ref[...] = dq * lax.rsqrt(msq + EPS) * gq_ref[...]
    dk = seg('dk', 128)
    msk = _dot2(dk * dk, ones32_ref[0:128, 0:128]) * (1.0 / DIFF_DH)
    kd_ref[...] = dk * lax.rsqrt(msk + EPS) * gk_ref[...]
    vd_ref[...] = seg('dv', 128)

    lane = lax.broadcasted_iota(jnp.int32, (1, LANE), 1)
    x1_head = ((lane & 63) >= MLA_NOPE) & ((lane & 63) < MLA_NOPE + MLA_ROPE // 2)
    x1_zero = lane < MLA_ROPE // 2
    cs = cs_ref[...]
    sn = sn_ref[...]

    mqa = seg('mqa', 256)
    msa = jnp.sum(mqa * mqa, axis=-1, keepdims=True) * (1.0 / MLA_QL)
    qa = (mqa * lax.rsqrt(msa + EPS) * gqa_ref[...]).astype(BF16)
    q = _dot(qa, wuq_ref[...])
    q = jnp.concatenate([_rope_lanes(q[:, :LANE], cs, sn, x1_head),
                         _rope_lanes(q[:, LANE:], cs, sn, x1_head)], axis=1)
    msq2 = _dot2(q * q, ones64_ref[...]) * (1.0 / MLA_DQK)
    qm = q * lax.rsqrt(msq2 + EPS) * gq48_ref[...]

    mkva = seg('mkva', 128)
    msc = jnp.mean(mkva * mkva, axis=-1, keepdims=True)
    c = mkva * lax.rsqrt(msc + EPS) * gkva_ref[...]
    c_ref[...] = c

    kr0 = _rope_lanes(seg('kr0', 128), cs0_ref[...], sn0_ref[...], x1_zero)
    kr16_ref[...] = kr0[:, :MLA_ROPE]

    if sample:
        kr128_ref[...] = kr0
        qg = qm * gk48_ref[...]
        qabs_ref[...] = _dot2(qg, wabs_ref[...])
        qr_ref[...] = _dot3(qg, selr_ref[...])
    else:
        qm_ref[...] = qm
        krt = seg('krt', 256)
        krt = jnp.concatenate([_rope_lanes(krt[:, :LANE], cs, sn, x1_head),
                               _rope_lanes(krt[:, LANE:], cs, sn, x1_head)], axis=1)
        kpre = _dot(c.astype(BF16), wukp_ref[...]) + krt
        msk2 = _dot2(kpre * kpre, ones64_ref[...]) * (1.0 / MLA_DQK)
        kmla_ref[...] = kpre * lax.rsqrt(msk2 + EPS) * gk48_ref[...]

    su_ref[...] = seg('su', 256)
    hq_ref[...] = seg('hq', 256)
    hf_ref[...] = seg('hf', 256)
    hi_ref[...] = seg('hi', 256)


def _inproj(x, lw, tabs, tm, n_tab, sample):
    t = x.shape[0]
    assert t % tm == 0
    const = lambda shape: pl.BlockSpec(shape, lambda i: (0,) * len(shape))
    row = lambda w: pl.BlockSpec((tm, w), lambda i: (i, 0))
    tab = pl.BlockSpec((tm, LANE), lambda i: (i % n_tab, 0))
    ins = [x, lw['g'], lw['wa'], lw['gq'], lw['gk'], lw['gqa'], lw['wuq'], lw['gq48'], lw['gkva'], lw['wukp'],
           lw['gk48'], tabs['cs'], tabs['sn'], tabs['cs0'], tabs['sn0'], lw['ones32'], lw['ones64']]
    in_specs = [row(D_MODEL), const((1, D_MODEL)), const((D_MODEL, NA)), const((1, 256)), const((1, 128)),
                const((1, 256)), const((256, 256)), const((1, 256)), const((1, 128)), const((128, 256)),
                const((1, 256)), tab, tab, tab, tab, const((256, 256)), const((256, 256))]
    if sample:
        ins += [lw['wabs'], lw['selr']]
        in_specs += [const((256, 512)), const((256, 512))]
        widths = [256, 128, 128, 128, MLA_ROPE, 128, 512, 512, 256, 256, 256, 256]
    else:
        widths = [256, 128, 128, 256, 128, 256, MLA_ROPE, 256, 256, 256, 256]
    return pl.pallas_call(
        functools.partial(_inproj_kernel, sample),
        grid=(t // tm,),
        in_specs=in_specs,
        out_specs=[row(w) for w in widths],
        out_shape=[jax.ShapeDtypeStruct((t, w), F32) for w in widths],
        compiler_params=pltpu.CompilerParams(dimension_semantics=("arbitrary",), vmem_limit_bytes=VMEM_LIMIT),
        name="inproj_sample" if sample else "inproj_prompt",
    )(*ins)


def _merge_kernel(x_ref, g_ref, wb_ref, b0_ref, b1_ref, b2_ref, b3_ref, wbr_ref, wout_ref, y_ref):
    x = x_ref[...]
    ms = jnp.mean(x * x, axis=-1, keepdims=True)
    xn = (x * lax.rsqrt(ms + EPS) * g_ref[...]).astype(BF16)
    m = jnp.zeros(x.shape, F32)
    for k, b_ref in enumerate((b0_ref, b1_ref, b2_ref, b3_ref)):
        gates = _dot(xn, wb_ref[:, k * BRANCH_W:(k + 1) * BRANCH_W])
        br = b_ref[...] * (gates * _sigmoid(gates))
        up = _dot(br.astype(BF16), wbr_ref[k])
        mg = _dot(xn, wb_ref[:, N_BRANCH * BRANCH_W + k * D_MODEL:N_BRANCH * BRANCH_W + (k + 1) * D_MODEL])
        m = m + _sigmoid(mg) * up
    y_ref[...] = x + _dot(m.astype(BF16), wout_ref[...])


def _merge(x, lw, branches, tm):
    t = x.shape[0]
    const = lambda shape: pl.BlockSpec(shape, lambda i: (0,) * len(shape))
    row = lambda w: pl.BlockSpec((tm, w), lambda i: (i, 0))
    nb = N_BRANCH * BRANCH_W + N_BRANCH * D_MODEL
    return pl.pallas_call(
        _merge_kernel,
        grid=(t // tm,),
        in_specs=[row(D_MODEL), const((1, D_MODEL)), const((D_MODEL, nb)), row(256), row(256), row(256), row(256),
                  const((N_BRANCH, BRANCH_W, D_MODEL)), const((D_MODEL, D_MODEL))],
        out_specs=row(D_MODEL),
        out_shape=jax.ShapeDtypeStruct((t, D_MODEL), F32),
        compiler_params=pltpu.CompilerParams(dimension_semantics=("arbitrary",), vmem_limit_bytes=VMEM_LIMIT),
        name="merge",
    )(x, lw['g'], lw['wb'], *branches, lw['wbr'], lw['wout'])


def _diff_slope(g, kvh):
    head = kvh * (DIFF_HEADS // DIFF_KVH) + g
    return 2.0 ** (-8.0 * (head + 1) / DIFF_HEADS)


LOG2E = 1.4426950408889634


def _diff_rows():
    return [(kvh, g, mp) for kvh in range(2) for g in range(2) for mp in range(2)]


def _fill_diff_q(qbig_ref, q, rows):
    lane = lax.broadcasted_iota(jnp.int32, (1, LANE), 1)
    for r, (kvh, g, mp) in enumerate(_diff_rows()):
        lo = kvh * 64 + mp * 32
        qbig_ref[r * rows:(r + 1) * rows, :] = jnp.where(
            (lane >= lo) & (lane < lo + DIFF_DH), q[:, g * LANE:(g + 1) * LANE], 0.0).astype(BF16)


def _diff_finish(o_of, lam, gsub, ones64, lam_init):
    lane = lax.broadcasted_iota(jnp.int32, (1, LANE), 1)
    outs = []
    for g in range(2):
        per_kvh = [o_of(kvh * 4 + g * 2) - lam * o_of(kvh * 4 + g * 2 + 1) for kvh in range(2)]
        og = jnp.where(lane < DIFF_DV, per_kvh[0], per_kvh[1])
        msq = _dot2(og * og, ones64) * (1.0 / DIFF_DV)
        outs.append(og * lax.rsqrt(msq + EPS) * gsub * (1.0 - lam_init))
    return outs


def _online_update(t, m_ref, r0, rows):
    m_old = m_ref[r0:r0 + rows, :]
    m_new = jnp.maximum(m_old, jnp.max(t, axis=-1, keepdims=True))
    m_ref[r0:r0 + rows, :] = m_new
    alpha = jnp.exp2(m_old - m_new)
    ps = [jnp.exp2(t[:, c * LANE:(c + 1) * LANE] - m_new) for c in range(t.shape[1] // LANE)]
    return alpha, ps


def _diffp_kernel(tq, tk, lam_init, lam_ref, q_ref, k_ref, v_ref, gsub_ref, ones64_ref, o_ref,
                  qbig_ref, kb_ref, vb_ref, p_ref, m_ref, acc_ref):
    qi = pl.program_id(1)
    lane = lax.broadcasted_iota(jnp.int32, (1, LANE), 1)

    @pl.when(qi == 0)
    def _():
        kb_ref[...] = k_ref[0].astype(BF16)
        v = v_ref[0]
        vb_ref[0] = jnp.where(lane < DIFF_DV, v, 1.0).astype(BF16)
        vb_ref[1] = jnp.where(lane < DIFF_DV, 1.0, v).astype(BF16)

    _fill_diff_q(qbig_ref, q_ref[0], tq)
    m_ref[...] = jnp.full(m_ref.shape, NEG, F32)
    acc_ref[...] = jnp.zeros(acc_ref.shape, F32)
    c1 = DIFF_DH ** -0.5 * LOG2E
    qpos = qi * tq + lax.broadcasted_iota(jnp.int32, (tq, 1), 0)

    def kv_step(j, masked):
        s_all = _dot_nt(qbig_ref[...], kb_ref[pl.ds(j * tk, tk), :])
        kpos = j * tk + lax.broadcasted_iota(jnp.int32, (1, tk), 1)
        kposf = kpos.astype(F32)
        for r, (kvh, g, mp) in enumerate(_diff_rows()):
            t = s_all[r * tq:(r + 1) * tq, :] * c1 + (_diff_slope(g, kvh) * LOG2E) * kposf
            if masked:
                t = jnp.where(kpos <= qpos, t, NEG)
            alpha, ps = _online_update(t, m_ref, r * tq, tq)
            acc_ref[r * tq:(r + 1) * tq, :] = alpha * acc_ref[r * tq:(r + 1) * tq, :]
            for c, p in enumerate(ps):
                p_ref[r * tq:(r + 1) * tq, c * LANE:(c + 1) * LANE] = p.astype(BF16)
        for kvh in range(2):
            rs = slice(kvh * 4 * tq, (kvh + 1) * 4 * tq)
            acc_ref[rs, :] = acc_ref[rs, :] + _dot(p_ref[rs, :], vb_ref[kvh, pl.ds(j * tk, tk), :])

    n_full = (qi * tq) // tk
    lax.fori_loop(0, n_full, lambda j, c: (kv_step(j, False), c)[1], 0)
    kv_step(n_full, True)

    def o_of(r):
        a = acc_ref[r * tq:(r + 1) * tq, :]
        return a / pltpu.roll(a, DIFF_DV, 1)

    outs = _diff_finish(o_of, lam_ref[0], gsub_ref[...], ones64_ref[...], lam_init)
    o_ref[0, :, 0:LANE] = outs[0]
    o_ref[0, :, LANE:2 * LANE] = outs[1]


def _diff_prompt(qd, kd, vd, lam, lw, lam_init, b, l):
    tq = min(ATTN_TQ, l)
    tk = min(ATTN_TK, l)
    assert tk % tq == 0 and l % tk == 0
    kern = functools.partial(_diffp_kernel, tq, tk, lam_init)
    return pl.pallas_call(
        kern,
        grid=(b, l // tq),
        in_specs=[pl.BlockSpec(memory_space=pltpu.SMEM),
                  pl.BlockSpec((1, tq, 256), lambda bi, qi: (bi, qi, 0)),
                  pl.BlockSpec((1, l, 128), lambda bi, qi: (bi, 0, 0)),
                  pl.BlockSpec((1, l, 128), lambda bi, qi: (bi, 0, 0)),
                  pl.BlockSpec((1, 128), lambda bi, qi: (0, 0)),
                  pl.BlockSpec((128, 128), lambda bi, qi: (0, 0))],
        out_specs=pl.BlockSpec((1, tq, 256), lambda bi, qi: (bi, qi, 0)),
        out_shape=jax.ShapeDtypeStruct((b, l, 256), F32),
        scratch_shapes=[pltpu.VMEM((8 * tq, LANE), BF16), pltpu.VMEM((l, LANE), BF16),
                        pltpu.VMEM((2, l, LANE), BF16), pltpu.VMEM((8 * tq, tk), BF16),
                        pltpu.VMEM((8 * tq, LANE), F32), pltpu.VMEM((8 * tq, LANE), F32)],
        compiler_params=pltpu.CompilerParams(dimension_semantics=("arbitrary", "arbitrary"),
                                             vmem_limit_bytes=VMEM_LIMIT),
        name="diff_prompt",
    )(lam, qd.reshape(b, l, 256), kd.reshape(b, l, 128), vd.reshape(b, l, 128), lw['gsub'], lw['ones64_128'])


def _fill_mla_q(qbig_ref, q, rows):
    lane = lax.broadcasted_iota(jnp.int32, (1, 2 * LANE), 1)
    for h in range(MLA_HEADS):
        qbig_ref[h * rows:(h + 1) * rows, :] = jnp.where(
            (lane >= h * 64) & (lane < (h + 1) * 64), q, 0.0).astype(BF16)


def _mla_finish(acc_ref, l_ref, wuv_ref, rows):
    o = jnp.zeros((rows, 2 * LANE), F32)
    for h in range(MLA_HEADS):
        den = jnp.sum(l_ref[h * rows:(h + 1) * rows, :], axis=-1, keepdims=True)
        olat = acc_ref[h * rows:(h + 1) * rows, :] / den
        o = o + _dot(olat.astype(BF16), wuv_ref[h])
    return o


def _mlap_kernel(tq, tk, q_ref, k_ref, c_ref, wuv_ref, o_ref, qbig_ref, kb_ref, cb_ref, p_ref, m_ref, l_ref,
                 acc_ref):
    qi = pl.program_id(1)

    @pl.when(qi == 0)
    def _():
        kb_ref[...] = k_ref[0].astype(BF16)
        cb_ref[...] = c_ref[0].astype(BF16)

    _fill_mla_q(qbig_ref, q_ref[0], tq)
    m_ref[...] = jnp.full(m_ref.shape, NEG, F32)
    l_ref[...] = jnp.zeros(l_ref.shape, F32)
    acc_ref[...] = jnp.zeros(acc_ref.shape, F32)
    c1 = MLA_DQK ** -0.5 * LOG2E
    qpos = qi * tq + lax.broadcasted_iota(jnp.int32, (tq, 1), 0)

    def kv_step(j, masked):
        s_all = _dot_nt(qbig_ref[...], kb_ref[pl.ds(j * tk, tk), :])
        kpos = j * tk + lax.broadcasted_iota(jnp.int32, (1, tk), 1)
        for h in range(MLA_HEADS):
            rs = slice(h * tq, (h + 1) * tq)
            t = s_all[rs, :] * c1
            if masked:
                t = jnp.where(kpos <= qpos, t, NEG)
            alpha, ps = _online_update(t, m_ref, h * tq, tq)
            acc_ref[rs, :] = alpha * acc_ref[rs, :]
            lsum = alpha * l_ref[rs, :]
            for c, p in enumerate(ps):
                lsum = lsum + p
                p_ref[rs, c * LANE:(c + 1) * LANE] = p.astype(BF16)
            l_ref[rs, :] = lsum
        acc_ref[...] = acc_ref[...] + _dot(p_ref[...], cb_ref[pl.ds(j * tk, tk), :])

    n_full = (qi * tq) // tk
    lax.fori_loop(0, n_full, lambda j, c: (kv_step(j, False), c)[1], 0)
    kv_step(n_full, True)
    o_ref[0] = _mla_finish(acc_ref, l_ref, wuv_ref, tq)


def _mla_prompt(qm, kmla, c, lw, b, l):
    tq = min(ATTN_TQ, l)
    tk = min(ATTN_TK, l)
    assert tk % tq == 0 and l % tk == 0
    nr = MLA_HEADS * tq
    return pl.pallas_call(
        functools.partial(_mlap_kernel, tq, tk),
        grid=(b, l // tq),
        in_specs=[pl.BlockSpec((1, tq, 256), lambda bi, qi: (bi, qi, 0)),
                  pl.BlockSpec((1, l, 256), lambda bi, qi: (bi, 0, 0)),
                  pl.BlockSpec((1, l, 128), lambda bi, qi: (bi, 0, 0)),
                  pl.BlockSpec((MLA_HEADS, 128, 256), lambda bi, qi: (0, 0, 0))],
        out_specs=pl.BlockSpec((1, tq, 256), lambda bi, qi: (bi, qi, 0)),
        out_shape=jax.ShapeDtypeStruct((b, l, 256), F32),
        scratch_shapes=[pltpu.VMEM((nr, 2 * LANE), BF16), pltpu.VMEM((l, 2 * LANE), BF16),
                        pltpu.VMEM((l, LANE), BF16), pltpu.VMEM((nr, tk), BF16),
                        pltpu.VMEM((nr, LANE), F32), pltpu.VMEM((nr, LANE), F32), pltpu.VMEM((nr, LANE), F32)],
        compiler_params=pltpu.CompilerParams(dimension_semantics=("arbitrary", "arbitrary"),
                                             vmem_limit_bytes=VMEM_LIMIT),
        name="mla_prompt",
    )(qm.reshape(b, l, 256), kmla.reshape(b, l, 256), c.reshape(b, l, 128), lw['wuv'])


def _sample_attn_kernel(ppc, nch, n_pages, ns, past_len, lam_init, layer, pt_ref, lam_ref,
                        qd_ref, qabs_ref, qr_ref, kn_ref, vn_ref, cn_ref, krn_ref,
                        wukt_ref, ones8_ref, wuv_ref, gsub_ref, ones64_ref,
                        ckt_hbm, cvt_hbm, cc_hbm, ckrt_hbm, od_ref, om_ref,
                        kbuf, vbuf, cbuf, krbuf, sems,
                        qd_s, qw_s, qr_s, m1, l1, acc1, m2, l2, acc2, pad_s):
    b = pl.program_id(0)
    last_chunk = pl.num_programs(0) * nch - 1
    nd = 8 * ns
    nm = MLA_HEADS * ns
    c1d = DIFF_DH ** -0.5 * LOG2E
    c1m = MLA_DQK ** -0.5 * LOG2E

    def page_copies(chunk, slot):
        base = chunk * ppc
        copies = []
        for j in range(ppc):
            pg = pt_ref[base + j]
            tok = pl.ds(j * PAGE, PAGE)
            for a, (src, dst) in enumerate(((ckt_hbm, kbuf.at[slot, :, tok]), (cvt_hbm, vbuf.at[slot, :, tok]),
                                            (cc_hbm, cbuf.at[slot, tok, :]), (ckrt_hbm, krbuf.at[slot, :, tok]))):
                copies.append(pltpu.make_async_copy(src.at[layer, pg], dst, sems.at[slot, a, j]))
        return copies

    def start_copies(chunk, slot):
        for k, cp in enumerate(page_copies(chunk, slot)):
            cp.start(priority=(k // 4) % 2)

    @pl.when(b == 0)
    def _():
        start_copies(0, 0)

    _fill_diff_q(qd_s, qd_ref[...], ns)
    for h in range(MLA_HEADS):
        qw_s[h * ns:(h + 1) * ns, :] = qabs_ref[:, h * LANE:(h + 1) * LANE].astype(BF16)
        qr_s[h * ns:(h + 1) * ns, :] = qr_ref[:, h * LANE:(h + 1) * LANE].astype(BF16)
    qw_s[nm:nm + LANE, :] = wukt_ref[...]
    m1[...] = jnp.full(m1.shape, NEG, F32)
    l1[...] = jnp.zeros(l1.shape, F32)
    acc1[...] = jnp.zeros(acc1.shape, F32)
    m2[...] = jnp.full(m2.shape, NEG, F32)
    l2[...] = jnp.zeros(l2.shape, F32)
    acc2[...] = jnp.zeros(acc2.shape, F32)

    rowd = lax.broadcasted_iota(jnp.int32, (nd, 1), 0)
    slope2 = jnp.zeros((nd, 1), F32)
    for r, (kvh, g, mp) in enumerate(_diff_rows()):
        slope2 = jnp.where((rowd >= r * ns) & (rowd < (r + 1) * ns), _diff_slope(g, kvh) * LOG2E, slope2)
    qrow_d = rowd & (ns - 1)
    qrow_m = lax.broadcasted_iota(jnp.int32, (nm, 1), 0) & (ns - 1)

    def mla_scores(r, rope_part, krsq):
        num = r[0:nm, :] + rope_part
        kt2 = r[nm:nm + LANE, :]
        sq = kt2 * kt2
        rows = []
        for h in range(MLA_HEADS):
            ssq = jnp.sum(sq[h * MLA_NOPE:(h + 1) * MLA_NOPE, :], axis=0, keepdims=True) + krsq
            inv = lax.rsqrt(ssq * (1.0 / MLA_DQK) + EPS) * c1m
            rows.append(num[h * ns:(h + 1) * ns, :] * inv)
        return jnp.concatenate(rows, axis=0)

    def accumulate(t, m_ref, l_ref, acc_ref, pv):
        alpha, ps = _online_update(t, m_ref, 0, t.shape[0])
        lsum = alpha * l_ref[...]
        for p in ps:
            lsum = lsum + p
        l_ref[...] = lsum
        p_all = ps[0] if len(ps) == 1 else jnp.concatenate(ps, axis=1)
        acc_ref[...] = alpha * acc_ref[...] + pv(p_all.astype(BF16))

    def chunk_step(ch, slot):
        chunk = b * nch + ch
        start_copies(jnp.minimum(chunk + 1, last_chunk), 1 - slot)
        for cp in page_copies(chunk, slot):
            cp.wait()
        n = ppc * PAGE
        kposf = (ch * n + lax.broadcasted_iota(jnp.int32, (1, n), 1)).astype(F32)
        t_d = _dot(qd_s[...], kbuf[slot].astype(BF16)) * c1d + slope2 * kposf
        accumulate(t_d, m1, l1, acc1, lambda p: _dot_nt(p, vbuf[slot].astype(BF16)))
        cb = cbuf[slot].astype(BF16)
        krt = krbuf[slot]
        rope_part = _dot(qr_s[:, 0:MLA_ROPE], krt.astype(BF16))
        t_m = mla_scores(_dot_nt(qw_s[...], cb), rope_part, jnp.sum(krt * krt, axis=0, keepdims=True))
        accumulate(t_m, m2, l2, acc2, lambda p: _dot(p, cb))

    def chunk_pair(k, carry):
        chunk_step(2 * k, 0)
        chunk_step(2 * k + 1, 1)
        return carry

    lax.fori_loop(0, nch // 2, chunk_pair, 0)

    jpos = lax.broadcasted_iota(jnp.int32, (1, PAGE), 1)
    pad_s[...] = jnp.zeros(pad_s.shape, F32)
    pad_s[0, 0:ns, :] = kn_ref[...]
    pad_s[1, 0:ns, :] = vn_ref[...]
    pad_s[2, 0:ns, :] = cn_ref[...]
    pad_s[3, 0:ns, :] = krn_ref[...]
    knb = pad_s[0].astype(BF16)
    vnb = pad_s[1].astype(BF16)
    cnb = pad_s[2].astype(BF16)
    krn = pad_s[3]
    t = _dot_nt(qd_s[...], knb) * c1d + slope2 * (past_len + jpos).astype(F32)
    accumulate(jnp.where(jpos <= qrow_d, t, NEG), m1, l1, acc1, lambda p: _dot(p, vnb))
    hi, lo = _split2(krn * krn)
    krsq = (_dot_nt(ones8_ref[...], hi) + _dot_nt(ones8_ref[...], lo))[0:1, :]
    t = mla_scores(_dot_nt(qw_s[...], cnb), _dot_nt(qr_s[...], krn.astype(BF16)), krsq)
    accumulate(jnp.where(jpos <= qrow_m, t, NEG), m2, l2, acc2, lambda p: _dot(p, cnb))

    def o_of(r):
        den = jnp.sum(l1[r * ns:(r + 1) * ns, :], axis=-1, keepdims=True)
        return acc1[r * ns:(r + 1) * ns, :] / den

    outs = _diff_finish(o_of, lam_ref[0], gsub_ref[...], ones64_ref[...], lam_init)
    od_ref[:, 0:LANE] = outs[0]
    od_ref[:, LANE:2 * LANE] = outs[1]
    om_ref[...] = _mla_finish(acc2, l2, wuv_ref, ns)

    @pl.when(b == pl.num_programs(0) - 1)
    def _():
        for cp in page_copies(last_chunk, 0):
            cp.wait()


def _sample_attn(layer, lam, lw, lam_init, page_table, caches, qd, qabs, qr, kn, vn, cn, krn, nb, ns):
    n_pages = page_table.shape[1]
    ppc = math.gcd(n_pages, PAGES_PER_CHUNK)
    nch = n_pages // ppc
    assert ns & (ns - 1) == 0 and nch % 2 == 0
    past_len = n_pages * PAGE
    pt_flat = page_table.reshape(-1)
    seq = lambda w: pl.BlockSpec((ns, w), lambda i, pt: (i, 0))
    const = lambda shape: pl.BlockSpec(shape, lambda i, pt: (0,) * len(shape))
    hbm = pl.BlockSpec(memory_space=pl.ANY)
    in_specs = [pl.BlockSpec(memory_space=pltpu.SMEM),
                seq(256), seq(512), seq(512), seq(128), seq(128), seq(128), seq(128),
                const((128, 128)), const((8, 128)), const((MLA_HEADS, 128, 256)),
                const((1, 128)), const((128, 128)), hbm, hbm, hbm, hbm]
    ins = [lam, qd, qabs, qr, kn, vn, cn, krn, lw['wukt'], lw['ones8'], lw['wuv'], lw['gsub'],
           lw['ones64_128'], *caches]
    nd, nm = 8 * ns, MLA_HEADS * ns
    kern = functools.partial(_sample_attn_kernel, ppc, nch, n_pages, ns, past_len, lam_init, layer)
    n = ppc * PAGE
    wide_buf = lambda rows: pltpu.VMEM((2, rows, n), F32)
    return pl.pallas_call(
        kern,
        grid_spec=pltpu.PrefetchScalarGridSpec(
            num_scalar_prefetch=1,
            grid=(nb,),
            in_specs=in_specs,
            out_specs=[seq(256), seq(256)],
            scratch_shapes=[wide_buf(PAGE), wide_buf(PAGE), pltpu.VMEM((2, n, LANE), F32), wide_buf(MLA_ROPE),
                            pltpu.SemaphoreType.DMA((2, 4, ppc)),
                            pltpu.VMEM((nd, LANE), BF16), pltpu.VMEM((nm + LANE, LANE), BF16),
                            pltpu.VMEM((nm, LANE), BF16),
                            pltpu.VMEM((nd, LANE), F32), pltpu.VMEM((nd, LANE), F32), pltpu.VMEM((nd, LANE), F32),
                            pltpu.VMEM((nm, LANE), F32), pltpu.VMEM((nm, LANE), F32), pltpu.VMEM((nm, LANE), F32),
                            pltpu.VMEM((4, PAGE, LANE), F32)]),
        out_shape=[jax.ShapeDtypeStruct((nb * ns, 256), F32), jax.ShapeDtypeStruct((nb * ns, 256), F32)],
        compiler_params=pltpu.CompilerParams(dimension_semantics=("arbitrary",), vmem_limit_bytes=VMEM_LIMIT),
        name="sample_attn",
    )(pt_flat, *ins)


N_ST = 2 * SSM_W // LANE


def _s5_pitch(tb):
    return tb + SUB if (tb // SUB) % 2 == 0 else tb


def _s5_kernel(tb, u_ref, h0re_ref, h0im_ref, bbar_ref, are_ref, aim_ref, cmat_ref, d_ref, wglu_ref,
               o_ref, hre_ref, him_ref, ubt, utm, bus, hs, otm, obt, hst):
    tblk = pl.program_id(1)
    pitch = _s5_pitch(tb)
    half = N_ST // 2
    nlt = BRANCH_W // LANE

    @pl.when(tblk == 0)
    def _():
        for j in range(half):
            hst[j] = h0re_ref[:, j * LANE:(j + 1) * LANE]
            hst[half + j] = h0im_ref[:, j * LANE:(j + 1) * LANE]

    for bi in range(SUB):
        for c in range(nlt):
            ubt[c, bi * pitch:bi * pitch + tb, :] = u_ref[bi, :, c * LANE:(c + 1) * LANE]

    def to_time_major(t, carry):
        r0 = pl.multiple_of(t * SUB, SUB)
        for c in range(nlt):
            utm[pl.ds(r0, SUB), c * LANE:(c + 1) * LANE] = ubt[c, pl.ds(t, SUB, stride=pitch), :]
        return carry

    lax.fori_loop(0, tb, to_time_major, 0)
    bus[...] = _dot(utm[...].astype(BF16), bbar_ref[...])
    ar = [jnp.broadcast_to(are_ref[:, j * LANE:(j + 1) * LANE], (SUB, LANE)) for j in range(half)]
    ai = [jnp.broadcast_to(aim_ref[:, j * LANE:(j + 1) * LANE], (SUB, LANE)) for j in range(half)]

    def step(t, st):
        r0 = pl.multiple_of(t * SUB, SUB)
        new = [None] * N_ST
        for j in range(half):
            hr, hi = st[j], st[half + j]
            nr = ar[j] * hr - ai[j] * hi + bus[pl.ds(r0, SUB), j * LANE:(j + 1) * LANE]
            ni = ar[j] * hi + ai[j] * hr + bus[pl.ds(r0, SUB), (half + j) * LANE:(half + j + 1) * LANE]
            hs[pl.ds(r0, SUB), j * LANE:(j + 1) * LANE] = nr
            hs[pl.ds(r0, SUB), (half + j) * LANE:(half + j + 1) * LANE] = ni
            new[j], new[half + j] = nr, ni
        return tuple(new)

    st = lax.fori_loop(0, tb, step, tuple(hst[j] for j in range(N_ST)))
    for j in range(N_ST):
        hst[j] = st[j]

    y = d_ref[...] * utm[...] + _dot(hs[...].astype(BF16), cmat_ref[...])
    yg = _dot(y.astype(BF16), wglu_ref[...])
    otm[...] = yg[:, :BRANCH_W] * _sigmoid(yg[:, BRANCH_W:])

    def to_batch_major(t, carry):
        r0 = pl.multiple_of(t * SUB, SUB)
        for c in range(nlt):
            obt[c, pl.ds(t, SUB, stride=pitch), :] = otm[pl.ds(r0, SUB), c * LANE:(c + 1) * LANE]
        return carry

    lax.fori_loop(0, tb, to_batch_major, 0)
    for bi in range(SUB):
        for c in range(nlt):
            o_ref[bi, :, c * LANE:(c + 1) * LANE] = obt[c, bi * pitch:bi * pitch + tb, :]

    @pl.when(tblk == pl.num_programs(1) - 1)
    def _():
        for j in range(half):
            hre_ref[:, j * LANE:(j + 1) * LANE] = st[j]
            him_ref[:, j * LANE:(j + 1) * LANE] = st[half + j]


def _s5(su, h0re, h0im, lw, b, l):
    assert b % SUB == 0
    tb = min(S5_TIME_BLOCK, l)
    rows = SUB * tb
    const = lambda shape: pl.BlockSpec(shape, lambda bi, ti: (0,) * len(shape))
    st_spec = pl.BlockSpec((SUB, SSM_W), lambda bi, ti: (bi, 0))
    return pl.pallas_call(
        functools.partial(_s5_kernel, tb),
        grid=(b // SUB, l // tb),
        in_specs=[pl.BlockSpec((SUB, tb, BRANCH_W), lambda bi, ti: (bi, ti, 0)), st_spec, st_spec,
                  const((BRANCH_W, 2 * SSM_W)), const((1, SSM_W)), const((1, SSM_W)),
                  const((2 * SSM_W, BRANCH_W)), const((1, BRANCH_W)), const((BRANCH_W, 2 * BRANCH_W))],
        out_specs=[pl.BlockSpec((SUB, tb, BRANCH_W), lambda bi, ti: (bi, ti, 0)), st_spec, st_spec],
        out_shape=[jax.ShapeDtypeStruct((b, l, BRANCH_W), F32), jax.ShapeDtypeStruct((b, SSM_W), F32),
                   jax.ShapeDtypeStruct((b, SSM_W), F32)],
        scratch_shapes=[pltpu.VMEM((BRANCH_W // LANE, SUB * _s5_pitch(tb), LANE), F32),
                        pltpu.VMEM((rows, BRANCH_W), F32),
                        pltpu.VMEM((rows, 2 * SSM_W), F32), pltpu.VMEM((rows, 2 * SSM_W), F32),
                        pltpu.VMEM((rows, BRANCH_W), F32),
                        pltpu.VMEM((BRANCH_W // LANE, SUB * _s5_pitch(tb), LANE), F32),
                        pltpu.VMEM((N_ST, SUB, LANE), F32)],
        compiler_params=pltpu.CompilerParams(dimension_semantics=("arbitrary", "arbitrary"),
                                             vmem_limit_bytes=VMEM_LIMIT),
        name="s5",
    )(su.reshape(b, l, BRANCH_W), h0re, h0im, lw['bbar'], lw['a_re'], lw['a_im'], lw['cmat'], lw['ssm_d'], lw['wglu'])


def _hgrn_kernel(chunk, nchunk, chain, q_ref, f_ref, v_ref, s0_ref, lb_ref, gain_ref, tri_ref, same_ref,
                 ones64_ref, o_ref, sfin_ref, kp, bcp, vp, st, oacc):
    rows = chunk * nchunk
    q = q_ref[0]
    zf = f_ref[0]
    v = v_ref[0]
    lb = lb_ref[...]
    logf = jnp.log(lb + (1.0 - lb) * _sigmoid(zf))
    kk = (1.0 - lb) * _sigmoid(-zf)
    bc = _dot3t(tri_ref[...], logf)
    bl = _dot3t(same_ref[...], logf)
    qh = q * jnp.exp(bc)
    kdec = (kk * jnp.exp(bl - bc)).astype(BF16)

    kp[0:HG_CHUNK, :] = jnp.zeros((HG_CHUNK, BRANCH_W), F32)
    bcp[0:HG_CHUNK, :] = jnp.zeros((HG_CHUNK, BRANCH_W), F32)
    vp[0:HG_CHUNK, :] = jnp.zeros((HG_CHUNK, BRANCH_W), F32)
    kp[HG_CHUNK:HG_CHUNK + rows, :] = kk
    bcp[HG_CHUNK:HG_CHUNK + rows, :] = bc
    vp[HG_CHUNK:HG_CHUNK + rows, :] = v
    rpos = lax.broadcasted_iota(jnp.int32, (rows, 1), 0) & (chunk - 1)
    o = _dot((q * kk).astype(BF16), ones64_ref[...]) * v
    for d in range(1, chunk):
        lo = HG_CHUNK - d
        ks = kp[lo:lo + rows, :]
        bcs = bcp[lo:lo + rows, :]
        vs = vp[lo:lo + rows, :]
        x = q * ks * jnp.exp(jnp.where(rpos >= d, bc - bcs, NEG))
        o = o + _dot(x.astype(BF16), ones64_ref[...]) * vs
    oacc[...] = o

    lane = lax.broadcasted_iota(jnp.int32, (1, BRANCH_W), 1)
    rowi = lax.broadcasted_iota(jnp.int32, (rows, 1), 0)
    head_lanes = [(lane >= h * HG_DK) & (lane < (h + 1) * HG_DK) for h in range(HG_HEADS)]
    kvs = []
    for i in range(nchunk):
        r0 = i * chunk
        vmask = jnp.where((rowi >= r0) & (rowi < r0 + chunk), v, 0.0).astype(BF16)
        kvt = _dot_tn(vmask, kdec)
        comp = jnp.zeros((HG_DV, BRANCH_W), F32)
        for h in range(HG_HEADS):
            comp = jnp.where(head_lanes[h], kvt[h * HG_DV:(h + 1) * HG_DV, :], comp)
        kvs.append(comp)
    if chain:
        @pl.when(pl.program_id(1) == 0)
        def _():
            st[...] = s0_ref[0]
        s_t = st[...]
    for i in range(nchunk):
        r0 = i * chunk
        if not chain:
            s_t = s0_ref[i]
        full = jnp.concatenate([jnp.where(head_lanes[h], s_t, 0.0) for h in range(HG_HEADS)],
                               axis=0).astype(BF16)
        oacc[r0:r0 + chunk, :] = oacc[r0:r0 + chunk, :] + _dot_nt(qh[r0:r0 + chunk, :].astype(BF16), full)
        s_t = s_t * jnp.exp(bl[r0:r0 + 1, :]) + kvs[i]
        if not chain:
            sfin_ref[i] = s_t
    if chain:
        st[...] = s_t
        sfin_ref[0] = s_t
    ot = oacc[...]
    msq = _dot2(ot * ot, ones64_ref[...]) * (1.0 / HG_DV)
    o_ref[0] = ot * lax.rsqrt(msq + EPS) * gain_ref[...]


def _dot3t(m, x):
    hi, mid, lo = _split3(x)
    return _dot(m, hi) + _dot(m, mid) + _dot(m, lo)


def _hgrn(hq, hf, hi, s0t, lw, b, l):
    chunk = math.gcd(l, HG_CHUNK)
    if l >= PAGE:
        chain, rows, nblk, outer = True, PAGE, l // PAGE, b
    else:
        assert l == chunk
        per = min(b, PAGE // l)
        chain, rows, nblk, outer = False, per * l, 1, b // per
    nchunk = rows // chunk
    sblk = 1 if chain else nchunk
    r = np.arange(rows)
    same = (r[:, None] // chunk) == (r[None, :] // chunk)
    tri = same & (r[None, :] <= r[:, None])
    blk = lambda: pl.BlockSpec((1, rows, BRANCH_W), lambda bi, ti: (bi * nblk + ti, 0, 0))
    const = lambda shape: pl.BlockSpec(shape, lambda bi, ti: (0,) * len(shape))
    st_spec = pl.BlockSpec((sblk, HG_DV, BRANCH_W), lambda bi, ti: (bi, 0, 0))
    shp = (outer * nblk, rows, BRANCH_W)
    o, sfin = pl.pallas_call(
        functools.partial(_hgrn_kernel, chunk, nchunk, chain),
        grid=(outer, nblk),
        in_specs=[blk(), blk(), blk(), st_spec, const((1, BRANCH_W)), const((1, BRANCH_W)),
                  const((rows, rows)), const((rows, rows)), const((BRANCH_W, BRANCH_W))],
        out_specs=[blk(), st_spec],
        out_shape=[jax.ShapeDtypeStruct(shp, F32), jax.ShapeDtypeStruct((b, HG_DV, BRANCH_W), F32)],
        scratch_shapes=[pltpu.VMEM((HG_CHUNK + rows, BRANCH_W), F32), pltpu.VMEM((HG_CHUNK + rows, BRANCH_W), F32),
                        pltpu.VMEM((HG_CHUNK + rows, BRANCH_W), F32), pltpu.VMEM((HG_DV, BRANCH_W), F32),
                        pltpu.VMEM((rows, BRANCH_W), F32)],
        compiler_params=pltpu.CompilerParams(dimension_semantics=("arbitrary", "arbitrary"),
                                             vmem_limit_bytes=VMEM_LIMIT),
        name="hgrn_chain" if chain else "hgrn_step",
    )(hq.reshape(shp), hf.reshape(shp), hi.reshape(shp), s0t, lw['lb'], lw['hg_gain'],
      jnp.asarray(tri, BF16), jnp.asarray(same, BF16), lw['ones64'])
    return o.reshape(b * l, BRANCH_W), sfin


def _block_ones(n, blk):
    r = np.arange(n)
    return jnp.asarray((r[:, None] // blk) == (r[None, :] // blk), BF16)


def _layout_indices():
    zero = IN_COLS
    a = np.full((NA,), zero, np.int64)
    for g in range(2):
        for kvh in range(2):
            for mp in range(2):
                dst = _A['dq'] + g * 128 + kvh * 64 + mp * 32
                src = _OFF['dq'] + kvh * 128 + g * 64 + mp * 32
                a[dst:dst + 32] = np.arange(src, src + 32)
    a[_A['dk']:_A['dk'] + 128] = np.arange(_OFF['dk'], _OFF['dk'] + 128)
    a[_A['dv']:_A['dv'] + 128] = np.arange(_OFF['dv'], _OFF['dv'] + 128)
    a[_A['mqa']:_A['mqa'] + MLA_QL] = np.arange(_OFF['mqa'], _OFF['mqa'] + MLA_QL)
    a[_A['mkva']:_A['mkva'] + 128] = np.arange(_OFF['mkva'], _OFF['mkva'] + 128)
    for h in range(MLA_HEADS):
        dst = _A['krt'] + h * 64 + MLA_NOPE
        a[dst:dst + MLA_ROPE] = np.arange(_OFF['mkr'], _OFF['mkr'] + MLA_ROPE)
    a[_A['kr0']:_A['kr0'] + MLA_ROPE] = np.arange(_OFF['mkr'], _OFF['mkr'] + MLA_ROPE)
    for name in ('su', 'hq', 'hf', 'hi'):
        a[_A[name]:_A[name] + 256] = np.arange(_OFF[name], _OFF[name] + 256)
    perm = np.zeros((256,), np.int64)
    for g in range(2):
        for kvh in range(2):
            perm[g * 128 + kvh * 64:g * 128 + kvh * 64 + 64] = np.arange(kvh * 128 + g * 64, kvh * 128 + g * 64 + 64)
    bcols = np.concatenate([_OFF['gates'] + perm, np.arange(_OFF['gates'] + 256, _OFF['gates'] + 1024),
                            np.arange(_OFF['merge'], _OFF['merge'] + N_BRANCH * D_MODEL)])
    return a, bcols, perm


_IDX_A, _IDX_B, _PERM_DIFF = _layout_indices()


def _head48(vec):
    return jnp.tile(jnp.pad(vec, (0, 64 - MLA_DQK)), MLA_HEADS).reshape(1, 256)


def _take_runs(w, idx, axis, zero):
    idx = np.asarray(idx)
    cuts = [0] + [i for i in range(1, len(idx))
                  if not ((idx[i] == zero and idx[i - 1] == zero) or
                          (idx[i] != zero and idx[i - 1] != zero and idx[i] == idx[i - 1] + 1))] + [len(idx)]
    pieces = []
    for a, b in zip(cuts[:-1], cuts[1:]):
        if idx[a] == zero:
            shape = list(w.shape)
            shape[axis] = b - a
            pieces.append(jnp.zeros(shape, w.dtype))
        else:
            pieces.append(lax.slice_in_dim(w, int(idx[a]), int(idx[a]) + (b - a), axis=axis))
    return jnp.concatenate(pieces, axis=axis)


def _prep_layer(p, l, lb_all):
    w_in = p['w_in'][l].astype(BF16)
    lw = {}
    lw['g'] = p['norm_gain'][l].reshape(1, D_MODEL)
    lw['wa'] = _take_runs(w_in, _IDX_A, 1, IN_COLS)
    lw['wb'] = _take_runs(w_in, _IDX_B, 1, IN_COLS)
    wbr = p['w_branch'][l]
    lw['wbr'] = jnp.concatenate([_take_runs(wbr[0], _PERM_DIFF, 0, -1)[None], wbr[1:]], axis=0).astype(BF16)
    lw['wout'] = p['w_out'][l].astype(BF16)
    lw['gq'] = jnp.tile(p['diff_q_gain'][l], 8).reshape(1, 256)
    lw['gk'] = jnp.tile(p['diff_k_gain'][l], 4).reshape(1, 128)
    lw['gsub'] = jnp.tile(p['diff_subln_gain'][l], 2).reshape(1, 128)
    lw['gqa'] = jnp.pad(p['mla_qa_gain'][l], (0, 256 - MLA_QL)).reshape(1, 256)
    wuq = p['w_mla_uq'][l].reshape(MLA_QL, MLA_HEADS, MLA_DQK)
    lw['wuq'] = jnp.pad(wuq, ((0, 256 - MLA_QL), (0, 0), (0, 64 - MLA_DQK))).reshape(256, 256).astype(BF16)
    lw['gq48'] = _head48(p['mla_q_gain'][l])
    lw['gk48'] = _head48(p['mla_k_gain'][l])
    lw['gkva'] = p['mla_kva_gain'][l].reshape(1, 128)
    wuk = p['w_mla_uk'][l]
    lw['wukp'] = jnp.pad(wuk, ((0, 0), (0, 0), (0, 64 - MLA_NOPE))).reshape(128, 256).astype(BF16)
    lw['wuk2d'] = wuk.reshape(128, MLA_HEADS * MLA_NOPE).astype(BF16)
    wabs = jnp.zeros((MLA_HEADS, 64, MLA_HEADS, 128), F32)
    selr = np.zeros((MLA_HEADS, 64, MLA_HEADS, 128), np.float32)
    for h in range(MLA_HEADS):
        wabs = wabs.at[h, :MLA_NOPE, h, :].set(wuk[:, h, :].T)
        selr[h, MLA_NOPE + np.arange(MLA_ROPE), h, np.arange(MLA_ROPE)] = 1.0
    lw['wabs'] = wabs.reshape(256, 512).astype(BF16)
    lw['selr'] = jnp.asarray(selr.reshape(256, 512), BF16)
    wuv = p['w_mla_uv'][l]
    wuvp = jnp.zeros((MLA_HEADS, 128, MLA_HEADS, MLA_DV), F32)
    for h in range(MLA_HEADS):
        wuvp = wuvp.at[h, :, h, :].set(wuv[:, h, :])
    lw['wuv'] = wuvp.reshape(MLA_HEADS, 128, 256).astype(BF16)
    lw['wukt'] = wuk.reshape(128, MLA_HEADS * MLA_NOPE).T.astype(BF16)
    ones8 = np.zeros((8, 128), np.float32)
    ones8[:, :MLA_ROPE] = 1.0
    lw['ones8'] = jnp.asarray(ones8, BF16)
    lw['ones32'] = _block_ones(256, 32)
    lw['ones64'] = _block_ones(256, 64)
    lw['ones64_128'] = _block_ones(128, 64)

    lre, lim = p['ssm_a_re'][l], p['ssm_a_im'][l]
    dt = jnp.exp(p['ssm_log_dt'][l])[:, None]
    mag = jnp.exp(lre * dt)
    are, aim = mag * jnp.cos(lim * dt), mag * jnp.sin(lim * dt)
    den = lre * lre + lim * lim
    cre = ((are - 1.0) * lre + aim * lim) / den
    cim = (aim * lre - (are - 1.0) * lim) / den
    bre, bim = p['ssm_b_re'][l], p['ssm_b_im'][l]
    bbre = cre[..., None] * bre - cim[..., None] * bim
    bbim = cre[..., None] * bim + cim[..., None] * bre
    eye = jnp.eye(SSM_GROUPS, dtype=F32)
    b_re = jnp.einsum('gph,gk->ghkp', bbre, eye).reshape(BRANCH_W, SSM_W)
    b_im = jnp.einsum('gph,gk->ghkp', bbim, eye).reshape(BRANCH_W, SSM_W)
    lw['bbar'] = jnp.concatenate([b_re, b_im], axis=1).astype(BF16)
    c_re = jnp.einsum('ghp,gk->gpkh', p['ssm_c_re'][l], eye).reshape(SSM_W, BRANCH_W)
    c_im = jnp.einsum('ghp,gk->gpkh', p['ssm_c_im'][l], eye).reshape(SSM_W, BRANCH_W)
    lw['cmat'] = jnp.concatenate([c_re, -c_im], axis=0).astype(BF16)
    lw['a_re'] = are.reshape(1, SSM_W)
    lw['a_im'] = aim.reshape(1, SSM_W)
    lw['ssm_d'] = p['ssm_d'][l].reshape(1, BRANCH_W)
    lw['wglu'] = p['w_glu'][l].astype(BF16)
    lw['lb'] = lb_all[l].reshape(1, BRANCH_W)
    lw['hg_gain'] = jnp.tile(p['hg_norm_gain'][l], HG_HEADS).reshape(1, BRANCH_W)
    lp = p['diff_lambda'][l]
    lam_init = 0.8 - 0.6 * math.exp(-0.3 * l)
    lw['lam'] = (jnp.exp(jnp.sum(lp[0] * lp[1])) - jnp.exp(jnp.sum(lp[2] * lp[3])) + lam_init).reshape(1)
    return lw, lam_init


def _rope_tables(pos):
    half = MLA_ROPE // 2
    freqs = ROPE_BASE ** (-jnp.arange(half, dtype=F32) / half)
    ang = pos.astype(F32)[:, None] * freqs
    cos, sin = jnp.cos(ang), jnp.sin(ang)
    n = pos.shape[0]
    one = jnp.ones((n, MLA_NOPE), F32)
    zero = jnp.zeros((n, MLA_NOPE), F32)
    pad1 = jnp.ones((n, 64 - MLA_DQK), F32)
    pad0 = jnp.zeros((n, 64 - MLA_DQK), F32)
    cs_h = jnp.concatenate([one, cos, cos, pad1], axis=1)
    sn_h = jnp.concatenate([zero, -sin, sin, pad0], axis=1)
    cs0 = jnp.concatenate([cos, cos, jnp.ones((n, LANE - MLA_ROPE), F32)], axis=1)
    sn0 = jnp.concatenate([-sin, sin, jnp.zeros((n, LANE - MLA_ROPE), F32)], axis=1)
    return dict(cs=jnp.tile(cs_h, (1, 2)), sn=jnp.tile(sn_h, (1, 2)), cs0=cs0, sn0=sn0)


def kernel(x_prompt, x_sample, cache_diff_k, cache_diff_v, cache_mla_c, cache_mla_kr, state_ssm_re, state_ssm_im, state_hgrn, page_table, norm_gain, w_in, w_branch, w_out, diff_q_gain, diff_k_gain, diff_lambda, diff_subln_gain, mla_qa_gain, mla_kva_gain, w_mla_uq, w_mla_uk, w_mla_uv, mla_q_gain, mla_k_gain, ssm_a_re, ssm_a_im, ssm_log_dt, ssm_b_re, ssm_b_im, ssm_c_re, ssm_c_im, ssm_d, w_glu, hg_lb_logits, hg_norm_gain):
    p = dict(norm_gain=norm_gain, w_in=w_in, w_branch=w_branch, w_out=w_out, diff_q_gain=diff_q_gain,
             diff_k_gain=diff_k_gain, diff_lambda=diff_lambda, diff_subln_gain=diff_subln_gain,
             mla_qa_gain=mla_qa_gain, mla_kva_gain=mla_kva_gain, w_mla_uq=w_mla_uq, w_mla_uk=w_mla_uk,
             w_mla_uv=w_mla_uv, mla_q_gain=mla_q_gain, mla_k_gain=mla_k_gain, ssm_a_re=ssm_a_re, ssm_a_im=ssm_a_im,
             ssm_log_dt=ssm_log_dt, ssm_b_re=ssm_b_re, ssm_b_im=ssm_b_im, ssm_c_re=ssm_c_re, ssm_c_im=ssm_c_im,
             ssm_d=ssm_d, w_glu=w_glu, hg_norm_gain=hg_norm_gain)
    depth = w_in.shape[0]
    bp, lp, _ = x_prompt.shape
    bs, ls, _ = x_sample.shape
    n_pages = page_table.shape[1]
    past_len = n_pages * PAGE
    tp, ts = bp * lp, bs * ls

    sm = jax.nn.softmax(hg_lb_logits.astype(F32), axis=0)
    lb_all = jnp.cumsum(sm, axis=0) - sm[0]

    tm_p = min(TOKEN_TILE, lp)
    tm_s = min(TOKEN_TILE, ts)
    assert tm_s % ls == 0
    tabs_p = _rope_tables(jnp.arange(lp))
    tabs_s = _rope_tables(past_len + (jnp.arange(tm_s) % ls))

    n_pool = cache_diff_k.shape[1]
    ckt = jnp.transpose(cache_diff_k, (0, 1, 3, 4, 2)).reshape(depth, n_pool, 128, PAGE)
    cvt = jnp.transpose(cache_diff_v, (0, 1, 3, 4, 2)).reshape(depth, n_pool, 128, PAGE)
    ckrt = jnp.transpose(cache_mla_kr, (0, 1, 3, 2))
    caches = (ckt, cvt, cache_mla_c, ckrt)

    hp = x_prompt.reshape(tp, D_MODEL)
    hs = x_sample.reshape(ts, D_MODEL)
    zeros_p = jnp.zeros((bp, SSM_W), F32)
    zeros_st = jnp.zeros((bp, HG_DV, BRANCH_W), F32)
    outs_p, outs_s = [], []
    for l in range(depth):
        lw, lam_init = _prep_layer(p, l, lb_all)
        (qd, kd, vd, qm, c, kmla, kr16, su, hq, hf, hi) = _inproj(hp, lw, tabs_p, tm_p, lp // tm_p, False)
        o_diff = _diff_prompt(qd, kd, vd, lw['lam'], lw, lam_init, bp, lp).reshape(tp, 256)
        o_mla = _mla_prompt(qm, kmla, c, lw, bp, lp).reshape(tp, 256)
        o_ssm, hre, him = _s5(su, zeros_p, zeros_p, lw, bp, lp)
        o_hg, sfin = _hgrn(hq, hf, hi, zeros_st, lw, bp, lp)
        hp = _merge(hp, lw, (o_diff, o_mla, o_ssm.reshape(tp, 256), o_hg), tm_p)
        outs_p.append((kd, vd, c, kr16, hre, him, sfin))
        (qd, kd, vd, c, kr16, kr128, qabs, qr, su, hq, hf, hi) = _inproj(hs, lw, tabs_s, tm_s, 1, True)
        o_diff, o_mla = _sample_attn(l, lw['lam'], lw, lam_init, page_table, caches,
                                     qd, qabs, qr, kd, vd, c, kr128, bs, ls)
        o_ssm, hre, him = _s5(su, state_ssm_re[l].reshape(bs, SSM_W), state_ssm_im[l].reshape(bs, SSM_W), lw, bs, ls)
        s0t = jnp.transpose(state_hgrn[l], (0, 3, 1, 2)).reshape(bs, HG_DV, BRANCH_W)
        o_hg, sfin = _hgrn(hq, hf, hi, s0t, lw, bs, ls)
        hs = _merge(hs, lw, (o_diff, o_mla, o_ssm.reshape(ts, 256), o_hg), tm_s)
        outs_s.append((kd, vd, c, kr16, hre, him, sfin))

    def stack(group, i):
        return jnp.stack([st[i] for st in group], axis=0)

    def unstate(s, b):
        return jnp.transpose(s.reshape(depth, b, HG_DV, HG_HEADS, HG_DK), (0, 1, 3, 4, 2))

    npg = lp // PAGE
    return (hp.reshape(bp, lp, D_MODEL), hs.reshape(bs, ls, D_MODEL),
            stack(outs_p, 0).reshape(depth, bp, npg, PAGE, DIFF_KVH, 2 * DIFF_DH),
            stack(outs_p, 1).reshape(depth, bp, npg, PAGE, DIFF_KVH, DIFF_DV),
            stack(outs_p, 2).reshape(depth, bp, npg, PAGE, MLA_KVL),
            stack(outs_p, 3).reshape(depth, bp, npg, PAGE, MLA_ROPE),
            stack(outs_p, 4).reshape(depth, bp, SSM_GROUPS, SSM_STATE),
            stack(outs_p, 5).reshape(depth, bp, SSM_GROUPS, SSM_STATE),
            unstate(stack(outs_p, 6), bp),
            stack(outs_s, 0).reshape(depth, bs, ls, DIFF_KVH, 2 * DIFF_DH),
            stack(outs_s, 1).reshape(depth, bs, ls, DIFF_KVH, DIFF_DV),
            stack(outs_s, 2).reshape(depth, bs, ls, MLA_KVL),
            stack(outs_s, 3).reshape(depth, bs, ls, MLA_ROPE),
            stack(outs_s, 4).reshape(depth, bs, SSM_GROUPS, SSM_STATE),
            stack(outs_s, 5).reshape(depth, bs, SSM_GROUPS, SSM_STATE),
            unstate(stack(outs_s, 6), bs))
```
